```python
import jax, jax.numpy as jnp
from jax import lax
import numpy as np

D_MODEL = 1024
BATCH = 8
SEQ = 2048
DEPTH = 4
DEC_BATCH = 128
DEC_SEQ = 4
PAST_LEN = 2048
PAGE_SIZE = 128

N_SSM = (DEPTH + 1) // 2
N_ATT = DEPTH // 2
NORM_EPS = 1e-5
D_FF = 4 * D_MODEL
NEG_INF = -1e30

S5_WIDTH = D_MODEL // 2
S5_GROUP = 16
S5_GROUPS = S5_WIDTH // S5_GROUP
S5_STATE = 64

SSD_INNER = D_MODEL
SSD_HEAD_DIM = 64
SSD_HEADS = SSD_INNER // SSD_HEAD_DIM
SSD_STATE = 128
SSD_GROUPS = 4
SSD_CONV = 4
SSD_CONV_DIM = SSD_INNER + 2 * SSD_GROUPS * SSD_STATE
SSD_CHUNK = 128
MIX_EVEN = S5_WIDTH + SSD_INNER
IN_EVEN = S5_WIDTH + SSD_INNER + SSD_CONV_DIM + SSD_HEADS

N_HEADS = 16
HEAD_DIM = D_MODEL // N_HEADS
N_KV = 2
HEADS_PER_KV = N_HEADS // N_KV
KV_W = N_KV * HEAD_DIM
ROPE_DIM = HEAD_DIM // 4
ROPE_THETA = 500000.0
ATT_SCALE = HEAD_DIM ** -0.5
CMP_LEN = 32
CMP_STRIDE = 16
CMP_HIDDEN = 2 * HEAD_DIM
SEL_LEN = 64
SEL_TOPK = 16
FORCE_BONUS = 1e4
WINDOW = 512
ATT_QBLOCK = 128
SEL_QBLOCK = 64
IN_ODD = N_HEADS * HEAD_DIM + 6 * KV_W + 3 * N_HEADS

kernel_name = 'hybrid_s5_ssd_nsa_decode_step'


def rms_norm(x, g):
    xf = x.astype(jnp.float32)
    y = xf * lax.rsqrt(jnp.mean(xf * xf, -1, keepdims=True) + NORM_EPS)
    return (y * g.astype(jnp.float32)).astype(x.dtype)


def _block(n, pref):
    return pref if n % pref == 0 else n


def masked_softmax(s, mask):
    p = jax.nn.softmax(jnp.where(mask, s, NEG_INF), axis=-1)
    return jnp.where(mask, p, 0.0)


def rope(x, pos):
    half = ROPE_DIM // 2
    inv = ROPE_THETA ** (-jnp.arange(half, dtype=jnp.float32) / half)
    ang = pos.astype(jnp.float32)[:, None] * inv
    cos, sin = jnp.cos(ang)[:, None, :], jnp.sin(ang)[:, None, :]
    xr = x[..., :ROPE_DIM].astype(jnp.float32)
    x1, x2 = xr[..., :half], xr[..., half:]
    rot = jnp.concatenate([x1 * cos - x2 * sin, x2 * cos + x1 * sin], -1)
    return jnp.concatenate([rot.astype(x.dtype), x[..., ROPE_DIM:]], -1)


def _cmul(ar, ai, br, bi):
    return ar * br - ai * bi, ar * bi + ai * br


def _s5_combine(e1, e2):
    a1r, a1i, b1r, b1i = e1
    a2r, a2i, b2r, b2i = e2
    ar, ai = _cmul(a2r, a2i, a1r, a1i)
    br, bi = _cmul(a2r, a2i, b1r, b1i)
    return ar, ai, br + b2r, bi + b2i


def s5_mixer(u, h0, a_re, a_im, log_dt, b_re, b_im, c_re, c_im, d_skip, glu_w, glu_b):
    f32 = jnp.float32
    bt, s, _ = u.shape
    ug = u.astype(f32).reshape(bt, s, S5_GROUPS, S5_GROUP)
    a_re = a_re.astype(f32)
    a_im = a_im.astype(f32)
    dt = jnp.exp(log_dt.astype(f32))[:, None]
    mag = jnp.exp(a_re * dt)
    abar_re, abar_im = mag * jnp.cos(a_im * dt), mag * jnp.sin(a_im * dt)
    den = a_re * a_re + a_im * a_im
    f_re = ((abar_re - 1.0) * a_re + abar_im * a_im) / den
    f_im = (abar_im * a_re - (abar_re - 1.0) * a_im) / den
    bbar_re, bbar_im = _cmul(f_re[..., None], f_im[..., None], b_re.astype(f32), b_im.astype(f32))
    bu_re = jnp.einsum('bsgc,gpc->sbgp', ug, bbar_re)
    bu_im = jnp.einsum('bsgc,gpc->sbgp', ug, bbar_im)
    shape_a = (s, 1, S5_GROUPS, S5_STATE)
    acr, aci, hr, hi = lax.associative_scan(
        _s5_combine,
        (jnp.broadcast_to(abar_re, shape_a), jnp.broadcast_to(abar_im, shape_a), bu_re, bu_im),
        axis=0)
    h0f = h0.astype(f32)
    dr, di = _cmul(acr, aci, h0f[None, ..., 0], h0f[None, ..., 1])
    hr = hr + dr
    hi = hi + di
    y = (jnp.einsum('sbgp,gcp->bsgc', hr, c_re.astype(f32))
         - jnp.einsum('sbgp,gcp->bsgc', hi, c_im.astype(f32)))
    y = y + d_skip.astype(f32).reshape(S5_GROUPS, S5_GROUP) * ug
    zg = jnp.einsum('bsgc,gce->bsge', y, glu_w.astype(f32)) + glu_b.astype(f32)
    out = zg[..., :S5_GROUP] * jax.nn.sigmoid(zg[..., S5_GROUP:])
    h_last = jnp.stack([hr[-1], hi[-1]], -1)
    return out.reshape(bt, s, S5_WIDTH), h_last


def _segsum_exp(a):
    t = a.shape[-1]
    cs = jnp.cumsum(a, -1)
    tril = np.tril(np.ones((t, t), dtype=bool))
    return jnp.exp(jnp.where(tril, cs[..., :, None] - cs[..., None, :], -jnp.inf))


def ssd_mixer(z, xbc, dt_raw, conv_buf, h0, conv_w, conv_b, dt_bias, a_log, d_skip, norm_g):
    f32 = jnp.float32
    bt, s, _ = xbc.shape
    xpad = jnp.concatenate([conv_buf.astype(xbc.dtype), xbc], 1)
    conv = conv_b + sum(xpad[:, k:k + s] * conv_w[k] for k in range(SSD_CONV))
    new_buf = xpad[:, s:]
    xbc_a = jax.nn.silu(conv.astype(f32))
    n_bc = SSD_GROUPS * SSD_STATE
    x = xbc_a[..., :SSD_INNER].reshape(bt, s, SSD_HEADS, SSD_HEAD_DIM)
    bm = xbc_a[..., SSD_INNER:SSD_INNER + n_bc]
    cm = xbc_a[..., SSD_INNER + n_bc:]
    dt = jax.nn.softplus(dt_raw.astype(f32) + dt_bias.astype(f32))
    a = -jnp.exp(a_log.astype(f32))
    q = _block(s, SSD_CHUNK)
    nc = s // q
    r = SSD_HEADS // SSD_GROUPS
    xdt = (x * dt[..., None]).reshape(bt, nc, q, SSD_GROUPS, r, SSD_HEAD_DIM)
    bm = bm.reshape(bt, nc, q, SSD_GROUPS, SSD_STATE)
    cm = cm.reshape(bt, nc, q, SSD_GROUPS, SSD_STATE)
    a_dt = (dt * a).reshape(bt, nc, q, SSD_GROUPS, r).transpose(0, 3, 4, 1, 2)
    a_cs = jnp.cumsum(a_dt, -1)
    lmat = _segsum_exp(a_dt)
    cb = jnp.einsum('bclgn,bcsgn->bcgls', cm, bm)
    y_diag = jnp.einsum('bcgls,bgrcls,bcsgrp->bclgrp', cb, lmat, xdt)
    decay = jnp.exp(a_cs[..., -1:] - a_cs)
    states = jnp.einsum('bclgn,bgrcl,bclgrp->bcgrpn', bm, decay, xdt)
    h0g = h0.astype(f32).reshape(bt, 1, SSD_GROUPS, r, SSD_HEAD_DIM, SSD_STATE)
    states = jnp.concatenate([h0g, states], 1)
    chunk_decay = _segsum_exp(jnp.pad(a_cs[..., -1], [(0, 0)] * 3 + [(1, 0)]))
    states = jnp.einsum('bgrzc,bcgrpn->bzgrpn', chunk_decay, states)
    y_off = jnp.einsum('bclgn,bcgrpn,bgrcl->bclgrp', cm, states[:, :-1], jnp.exp(a_cs))
    y = (y_diag + y_off).reshape(bt, s, SSD_HEADS, SSD_HEAD_DIM) + x * d_skip.astype(f32)[:, None]
    y = y.reshape(bt, s, SSD_INNER) * jax.nn.silu(z.astype(f32))
    y = rms_norm(y, norm_g)
    h_last = states[:, -1].reshape(bt, SSD_HEADS, SSD_HEAD_DIM, SSD_STATE)
    return y, h_last, new_buf


def ssm_block(xn, s5_h0, ssd_h0, conv_buf, w_in, a_re, a_im, log_dt, b_re, b_im, c_re, c_im,
              d5, glu_w, glu_b, conv_w, conv_b, dt_bias, a_log, d_ssd, norm_g, w_out):
    proj = xn @ w_in
    o1 = S5_WIDTH
    o2 = o1 + SSD_INNER
    o3 = o2 + SSD_CONV_DIM
    u, z, xbc, dt_raw = proj[..., :o1], proj[..., o1:o2], proj[..., o2:o3], proj[..., o3:]
    ya, s5_h = s5_mixer(u, s5_h0, a_re, a_im, log_dt, b_re, b_im, c_re, c_im, d5, glu_w, glu_b)
    yb, ssd_h, conv_new = ssd_mixer(z, xbc, dt_raw, conv_buf, ssd_h0, conv_w, conv_b, dt_bias, a_log, d_ssd, norm_g)
    y = jnp.concatenate([ya, yb.astype(ya.dtype)], -1).astype(xn.dtype) @ w_out
    return y, s5_h, ssd_h, conv_new


def compress_blocks(k, w1, w2, pe):
    bt, t = k.shape[:2]
    ratio = CMP_LEN // CMP_STRIDE
    n_chunk = t // CMP_STRIDE
    n_cmp = n_chunk - ratio + 1
    ch = k[:, :n_chunk * CMP_STRIDE].reshape(bt, n_chunk, CMP_STRIDE, N_KV, HEAD_DIM)
    blocks = jnp.concatenate([ch[:, j:j + n_cmp] for j in range(ratio)], axis=2)
    blocks = blocks + pe[:, None, :]
    flat = blocks.transpose(0, 1, 3, 2, 4).reshape(bt, n_cmp, N_KV, CMP_LEN * HEAD_DIM)
    return jax.nn.silu(flat @ w1) @ w2


def nsa_compressed(q, full_cmp, q_off, w1, w2, pe):
    s = q.shape[1]
    kc = compress_blocks(full_cmp[:, :, 0], w1[0], w2[0], pe[0])
    vc = compress_blocks(full_cmp[:, :, 1], w1[1], w2[1], pe[1])
    n_cmp = kc.shape[1]
    q_pos = q_off + np.arange(s)
    ends = np.arange(n_cmp) * CMP_STRIDE + CMP_LEN - 1
    mask = ends[None, :] <= q_pos[:, None]
    sc = jnp.einsum('bsgrd,bngd->bsgrn', q, kc).astype(jnp.float32) * ATT_SCALE
    p = masked_softmax(sc, mask[None, :, None, None, :])
    return jnp.einsum('bsgrn,bngd->bsgrd', p, vc.astype(jnp.float32)), p


def nsa_selected(q, full_sel, p_cmp, q_off):
    bt, s = q.shape[:2]
    t = full_sel.shape[1]
    n_slc = -(-t // SEL_LEN)
    n_cmp = p_cmp.shape[-1]
    ci = np.arange(n_cmp)[:, None]
    sj = np.arange(n_slc)[None, :]
    overlap = ((ci * CMP_STRIDE < (sj + 1) * SEL_LEN)
               & (ci * CMP_STRIDE + CMP_LEN > sj * SEL_LEN)).astype(np.float32)
    imp = jnp.einsum('bsgn,nj->bsgj', p_cmp.sum(3), overlap)
    q_pos = q_off + np.arange(s)
    q_blk = q_pos // SEL_LEN
    jj = np.arange(n_slc)[None, :]
    valid = jj * SEL_LEN <= q_pos[:, None]
    forced = (jj == 0) | (jj == q_blk[:, None]) | (jj == q_blk[:, None] - 1)
    score = jnp.where(valid[None, :, None, :],
                      imp + np.where(forced, FORCE_BONUS, 0.0).astype(np.float32)[None, :, None, :],
                      NEG_INF)
    kk = min(SEL_TOPK, n_slc)
    _, idx = lax.top_k(score, kk)
    kv = jnp.pad(full_sel, ((0, 0), (0, n_slc * SEL_LEN - t), (0, 0), (0, 0), (0, 0)))
    kv = kv.reshape(bt, n_slc, SEL_LEN, 2, N_KV, HEAD_DIM).transpose(0, 4, 1, 2, 3, 5)
    kt, vt = kv[..., 0, :], kv[..., 1, :]
    qb = _block(s, SEL_QBLOCK)
    nb = s // qb
    q_blocks = q.reshape(bt, nb, qb, N_KV, HEADS_PER_KV, HEAD_DIM).swapaxes(0, 1)
    idx_blocks = idx.reshape(bt, nb, qb, N_KV, kk).swapaxes(0, 1)
    pos_blocks = jnp.asarray(q_pos.reshape(nb, qb))
    bi = jnp.arange(bt)[:, None, None, None]
    gi = jnp.arange(N_KV)[None, None, :, None]
    offs = jnp.arange(SEL_LEN)

    def one_block(args):
        qblk, iblk, pblk = args
        kg = kt[bi, gi, iblk]
        vg = vt[bi, gi, iblk]
        kpos = iblk[..., None] * SEL_LEN + offs
        mask = (kpos <= pblk[None, :, None, None, None])[:, :, :, None]
        sc = jnp.einsum('bqgrd,bqgkld->bqgrkl', qblk, kg).astype(jnp.float32) * ATT_SCALE
        shp = sc.shape
        p = masked_softmax(sc.reshape(shp[:4] + (kk * SEL_LEN,)),
                           jnp.broadcast_to(mask, shp).reshape(shp[:4] + (kk * SEL_LEN,))).reshape(shp)
        return jnp.einsum('bqgrkl,bqgkld->bqgrd', p, vg.astype(jnp.float32))

    o = lax.map(one_block, (q_blocks, idx_blocks, pos_blocks))
    return o.swapaxes(0, 1).reshape(bt, s, N_KV, HEADS_PER_KV, HEAD_DIM)


def nsa_window(q, full_win, n_prev):
    bt, s = q.shape[:2]
    qb = _block(s, ATT_QBLOCK)
    nb = s // qb
    band = qb + WINDOW - 1
    kv = jnp.pad(full_win, ((0, 0), (WINDOW - 1, 0), (0, 0), (0, 0), (0, 0)))
    q_blocks = q.reshape(bt, nb, qb, N_KV, HEADS_PER_KV, HEAD_DIM).swapaxes(0, 1)

    def one_block(args):
        blk, qblk = args
        start = blk * qb + n_prev
        kvb = lax.dynamic_slice_in_dim(kv, start, band, axis=1)
        k_idx = start - (WINDOW - 1) + jnp.arange(band)
        q_idx = start + jnp.arange(qb)
        rel = q_idx[:, None] - k_idx[None, :]
        mask = (k_idx[None, :] >= 0) & (rel >= 0) & (rel < WINDOW)
        sc = jnp.einsum('bqgrd,blgd->bqgrl', qblk, kvb[:, :, 0]).astype(jnp.float32) * ATT_SCALE
        p = masked_softmax(sc, mask[None, :, None, None, :])
        return jnp.einsum('bqgrl,blgd->bqgrd', p, kvb[:, :, 1].astype(jnp.float32))

    o = lax.map(one_block, (jnp.arange(nb), q_blocks))
    return o.swapaxes(0, 1).reshape(bt, s, N_KV, HEADS_PER_KV, HEAD_DIM)


def _heads(t, n):
    return t.reshape(t.shape[0], t.shape[1], n, HEAD_DIM)


def nsa_block(xn, q_off, past_cmp, past_sel, win_buf, w_in, cmp_w1, cmp_w2, cmp_pos, w_out):
    bt, s, _ = xn.shape
    q_w = N_HEADS * HEAD_DIM
    cuts = np.cumsum([q_w] + [KV_W] * 6).tolist()
    q, k_cmp, v_cmp, k_sel, v_sel, k_win, v_win, g = jnp.split(xn @ w_in, cuts, axis=-1)
    pos = q_off + jnp.arange(s)
    q = rope(_heads(q, N_HEADS), pos).reshape(bt, s, N_KV, HEADS_PER_KV, HEAD_DIM)
    new_cmp = jnp.stack([rope(_heads(k_cmp, N_KV), pos), _heads(v_cmp, N_KV)], 2)
    new_sel = jnp.stack([rope(_heads(k_sel, N_KV), pos), _heads(v_sel, N_KV)], 2)
    new_win = jnp.stack([rope(_heads(k_win, N_KV), pos), _heads(v_win, N_KV)], 2)
    full_cmp = jnp.concatenate([past_cmp.astype(new_cmp.dtype), new_cmp], 1)
    full_sel = jnp.concatenate([past_sel.astype(new_sel.dtype), new_sel], 1)
    full_win = jnp.concatenate([win_buf.astype(new_win.dtype), new_win], 1)
    o_cmp, p_cmp = nsa_compressed(q, full_cmp, q_off, cmp_w1, cmp_w2, cmp_pos)
    o_sel = nsa_selected(q, full_sel, p_cmp, q_off)
    o_win = nsa_window(q, full_win, win_buf.shape[1])
    gate = jax.nn.sigmoid(g.astype(jnp.float32)).reshape(bt, s, N_KV, HEADS_PER_KV, 3)
    o = gate[..., 0:1] * o_cmp + gate[..., 1:2] * o_sel + gate[..., 2:3] * o_win
    y = o.reshape(bt, s, q_w).astype(xn.dtype) @ w_out
    keep = min(WINDOW, full_win.shape[1])
    return y, new_cmp, new_sel, full_win[:, full_win.shape[1] - keep:]


def sq_relu_mlp(xn, w_up, w_down):
    h = jax.nn.relu(xn @ w_up)
    return (h * h) @ w_down


def setup_inputs(seed: int = 0) -> dict:
    key = jax.random.key(seed)
    ks = iter(jax.random.split(key, 64))
    f32 = jnp.float32
    nrm = lambda shape, std=1.0: jax.random.normal(next(ks), shape, f32) * std
    n_pages = PAST_LEN // PAGE_SIZE
    n_pool = (DEC_BATCH * n_pages * 5) // 4
    wbuf = min(WINDOW, PAST_LEN)
    page_table = jax.random.permutation(next(ks), n_pool)[:DEC_BATCH * n_pages].reshape(DEC_BATCH, n_pages).astype(jnp.int32)
    dt_ssd = jnp.exp(jax.random.uniform(next(ks), (N_SSM, SSD_HEADS), f32, np.log(1e-3), np.log(1e-1)))
    return {
        'x_prompt': nrm((BATCH, SEQ, D_MODEL)),
        'x_sample': nrm((DEC_BATCH, DEC_SEQ, D_MODEL)),
        'state_s5': nrm((N_SSM, DEC_BATCH, S5_GROUPS, S5_STATE, 2), 0.1),
        'state_ssd': nrm((N_SSM, DEC_BATCH, SSD_HEADS, SSD_HEAD_DIM, SSD_STATE), 0.1),
        'state_conv': nrm((N_SSM, DEC_BATCH, SSD_CONV - 1, SSD_CONV_DIM)),
        'cache_cmp_kv': nrm((N_ATT, n_pool, PAGE_SIZE, 2, N_KV, HEAD_DIM)),
        'cache_sel_kv': nrm((N_ATT, n_pool, PAGE_SIZE, 2, N_KV, HEAD_DIM)),
        'state_win_kv': nrm((N_ATT, DEC_BATCH, wbuf, 2, N_KV, HEAD_DIM)),
        'page_table': page_table,
        'norm_mix_even': 1.0 + nrm((N_SSM, D_MODEL), 0.01),
        'w_in_even': nrm((N_SSM, D_MODEL, IN_EVEN), D_MODEL ** -0.5),
        's5_a_re': -0.5 + nrm((N_SSM, S5_GROUPS, S5_STATE), 0.01),
        's5_a_im': jnp.pi * jnp.arange(S5_STATE, dtype=f32) + nrm((N_SSM, S5_GROUPS, S5_STATE), 0.01),
        's5_log_dt': jax.random.uniform(next(ks), (N_SSM, S5_GROUPS), f32, np.log(1e-3), np.log(1e-1)),
        's5_b_re': nrm((N_SSM, S5_GROUPS, S5_STATE, S5_GROUP), (2 * S5_GROUP) ** -0.5),
        's5_b_im': nrm((N_SSM, S5_GROUPS, S5_STATE, S5_GROUP), (2 * S5_GROUP) ** -0.5),
        's5_c_re': nrm((N_SSM, S5_GROUPS, S5_GROUP, S5_STATE), S5_STATE ** -0.5),
        's5_c_im': nrm((N_SSM, S5_GROUPS, S5_GROUP, S5_STATE), S5_STATE ** -0.5),
        's5_d': nrm((N_SSM, S5_WIDTH)),
        's5_glu_w': nrm((N_SSM, S5_GROUPS, S5_GROUP, 2 * S5_GROUP), S5_GROUP ** -0.5),
        's5_glu_b': nrm((N_SSM, S5_GROUPS, 2 * S5_GROUP), 0.01),
        'ssd_conv_w': nrm((N_SSM, SSD_CONV, SSD_CONV_DIM), 0.5),
        'ssd_conv_b': nrm((N_SSM, SSD_CONV_DIM), 0.01),
        'ssd_dt_bias': dt_ssd + jnp.log(-jnp.expm1(-dt_ssd)),
        'ssd_a_log': jnp.log(jax.random.uniform(next(ks), (N_SSM, SSD_HEADS), f32, 1.0, 16.0)),
        'ssd_d': 1.0 + nrm((N_SSM, SSD_HEADS), 0.1),
        'ssd_norm': 1.0 + nrm((N_SSM, SSD_INNER), 0.01),
        'w_out_even': nrm((N_SSM, MIX_EVEN, D_MODEL), MIX_EVEN ** -0.5),
        'norm_mix_odd': 1.0 + nrm((N_ATT, D_MODEL), 0.01),
        'w_in_odd': nrm((N_ATT, D_MODEL, IN_ODD), D_MODEL ** -0.5),
        'cmp_w1': nrm((N_ATT, 2, CMP_LEN * HEAD_DIM, CMP_HIDDEN), (CMP_LEN * HEAD_DIM) ** -0.5),
        'cmp_w2': nrm((N_ATT, 2, CMP_HIDDEN, HEAD_DIM), CMP_HIDDEN ** -0.5),
        'cmp_pos': nrm((N_ATT, 2, CMP_LEN, HEAD_DIM), 0.1),
        'w_out_odd': nrm((N_ATT, N_HEADS * HEAD_DIM, D_MODEL), (N_HEADS * HEAD_DIM) ** -0.5),
        'norm_mlp': 1.0 + nrm((DEPTH, D_MODEL), 0.01),
        'w_up': nrm((DEPTH, D_MODEL, D_FF), D_MODEL ** -0.5),
        'w_down': nrm((DEPTH, D_FF, D_MODEL), D_FF ** -0.5),
        'norm_final': 1.0 + nrm((D_MODEL,), 0.01),
    }


def reference(x_prompt, x_sample, state_s5, state_ssd, state_conv, cache_cmp_kv, cache_sel_kv,
              state_win_kv, page_table, norm_mix_even, w_in_even, s5_a_re, s5_a_im, s5_log_dt,
              s5_b_re, s5_b_im, s5_c_re, s5_c_im, s5_d, s5_glu_w, s5_glu_b, ssd_conv_w, ssd_conv_b,
              ssd_dt_bias, ssd_a_log, ssd_d, ssd_norm, w_out_even, norm_mix_odd, w_in_odd, cmp_w1,
              cmp_w2, cmp_pos, w_out_odd, norm_mlp, w_up, w_down, norm_final):
    bp, bs = x_prompt.shape[0], x_sample.shape[0]
    past_len = page_table.shape[1] * PAGE_SIZE
    dt_ = x_prompt.dtype
    hp, hs = x_prompt, x_sample
    s5_p, s5_s, ssd_p, ssd_s, conv_p, conv_s = [], [], [], [], [], []
    cmp_p, cmp_s, sel_p, sel_s, win_p, win_s = [], [], [], [], [], []
    for layer in range(DEPTH):
        i = layer // 2
        if layer % 2 == 0:
            w = (w_in_even[i], s5_a_re[i], s5_a_im[i], s5_log_dt[i], s5_b_re[i], s5_b_im[i],
                 s5_c_re[i], s5_c_im[i], s5_d[i], s5_glu_w[i], s5_glu_b[i], ssd_conv_w[i],
                 ssd_conv_b[i], ssd_dt_bias[i], ssd_a_log[i], ssd_d[i], ssd_norm[i], w_out_even[i])
            y, a, b, c = ssm_block(rms_norm(hp, norm_mix_even[i]),
                                   jnp.zeros((bp, S5_GROUPS, S5_STATE, 2), dt_),
                                   jnp.zeros((bp, SSD_HEADS, SSD_HEAD_DIM, SSD_STATE), dt_),
                                   jnp.zeros((bp, SSD_CONV - 1, SSD_CONV_DIM), dt_), *w)
            hp = hp + y.astype(hp.dtype)
            s5_p.append(a)
            ssd_p.append(b)
            conv_p.append(c)
            y, a, b, c = ssm_block(rms_norm(hs, norm_mix_even[i]), state_s5[i], state_ssd[i], state_conv[i], *w)
            hs = hs + y.astype(hs.dtype)
            s5_s.append(a)
            ssd_s.append(b)
            conv_s.append(c)
        else:
            w = (w_in_odd[i], cmp_w1[i], cmp_w2[i], cmp_pos[i], w_out_odd[i])
            empty = jnp.zeros((bp, 0, 2, N_KV, HEAD_DIM), dt_)
            y, a, b, c = nsa_block(rms_norm(hp, norm_mix_odd[i]), 0, empty, empty, empty, *w)
            hp = hp + y.astype(hp.dtype)
            cmp_p.append(a)
            sel_p.append(b)
            win_p.append(c)
            past_cmp = cache_cmp_kv[i][page_table].reshape(bs, past_len, 2, N_KV, HEAD_DIM)
            past_sel = cache_sel_kv[i][page_table].reshape(bs, past_len, 2, N_KV, HEAD_DIM)
            y, a, b, c = nsa_block(rms_norm(hs, norm_mix_odd[i]), past_len, past_cmp, past_sel, state_win_kv[i], *w)
            hs = hs + y.astype(hs.dtype)
            cmp_s.append(a)
            sel_s.append(b)
            win_s.append(c)
        hp = hp + sq_relu_mlp(rms_norm(hp, norm_mlp[layer]), w_up[layer], w_down[layer])
        hs = hs + sq_relu_mlp(rms_norm(hs, norm_mlp[layer]), w_up[layer], w_down[layer])
    y_prompt = rms_norm(hp, norm_final)
    y_sample = rms_norm(hs, norm_final)
    return (y_prompt, y_sample, jnp.stack(s5_p), jnp.stack(s5_s), jnp.stack(ssd_p), jnp.stack(ssd_s),
            jnp.stack(conv_p), jnp.stack(conv_s), jnp.stack(cmp_p), jnp.stack(cmp_s),
            jnp.stack(sel_p), jnp.stack(sel_s), jnp.stack(win_p), jnp.stack(win_s))
```

```python
import functools

import jax
import jax.numpy as jnp
import numpy as np
from jax import lax
from jax.experimental import pallas as pl
from jax.experimental.pallas import tpu as pltpu

D_MODEL = 1024
BATCH = 8
SEQ = 2048
DEPTH = 4
DEC_BATCH = 128
DEC_SEQ = 4
PAST_LEN = 2048
PAGE_SIZE = 128

N_SSM = (DEPTH + 1) // 2
N_ATT = DEPTH // 2
NORM_EPS = 1e-5
D_FF = 4 * D_MODEL
NEG_INF = -1e30

S5_WIDTH = D_MODEL // 2
S5_GROUP = 16
S5_GROUPS = S5_WIDTH // S5_GROUP
S5_STATE = 64

SSD_INNER = D_MODEL
SSD_HEAD_DIM = 64
SSD_HEADS = SSD_INNER // SSD_HEAD_DIM
SSD_STATE = 128
SSD_GROUPS = 4
SSD_CONV = 4
SSD_CONV_DIM = SSD_INNER + 2 * SSD_GROUPS * SSD_STATE
SSD_CHUNK = 128
MIX_EVEN = S5_WIDTH + SSD_INNER
IN_EVEN = S5_WIDTH + SSD_INNER + SSD_CONV_DIM + SSD_HEADS

N_HEADS = 16
HEAD_DIM = D_MODEL // N_HEADS
N_KV = 2
HEADS_PER_KV = N_HEADS // N_KV
KV_W = N_KV * HEAD_DIM
ROPE_DIM = HEAD_DIM // 4
ROPE_THETA = 500000.0
ATT_SCALE = HEAD_DIM ** -0.5
CMP_LEN = 32
CMP_STRIDE = 16
CMP_HIDDEN = 2 * HEAD_DIM
SEL_LEN = 64
SEL_TOPK = 16
FORCE_BONUS = 1e4
WINDOW = 512
ATT_QBLOCK = 128
SEL_QBLOCK = 64
IN_ODD = N_HEADS * HEAD_DIM + 6 * KV_W + 3 * N_HEADS

T_PROMPT = BATCH * SEQ
T_SAMPLE = DEC_BATCH * DEC_SEQ
T_ALL = T_PROMPT + T_SAMPLE

LANE = 128
TOKEN_TILE = 512
FF_CHUNK = 1024
VMEM_LIMIT = 56 * 1024 * 1024

F32 = jnp.float32
BF16 = jnp.bfloat16


def _pad_lanes(n):
    return -(-n // LANE) * LANE


def _rms(x, g):
    return x * lax.rsqrt(jnp.mean(x * x, -1, keepdims=True) + NORM_EPS) * g


def _resident(shape):
    return pl.BlockSpec(shape, lambda i: (0,) * len(shape), pipeline_mode=pl.Buffered(1))


def _rows(width):
    return pl.BlockSpec((TOKEN_TILE, width), lambda i: (i, 0))


_PARAMS = pltpu.CompilerParams(dimension_semantics=("arbitrary",), vmem_limit_bytes=VMEM_LIMIT)


def _norm_proj_body(x_ref, g_ref, w_ref, *o_refs, widths):
    xn = _rms(x_ref[...], g_ref[...]).astype(BF16)
    off = 0
    for o_ref, wd in zip(o_refs, widths):
        o_ref[...] = jnp.dot(xn, w_ref[:, off:off + wd], preferred_element_type=F32)
        off += wd


def norm_proj(x, g, w, widths):
    n = sum(widths)
    return pl.pallas_call(
        functools.partial(_norm_proj_body, widths=widths),
        grid=(T_ALL // TOKEN_TILE,),
        in_specs=[_rows(D_MODEL), _resident((1, D_MODEL)), _resident((D_MODEL, n))],
        out_specs=[_rows(wd) for wd in widths],
        out_shape=[jax.ShapeDtypeStruct((T_ALL, wd), F32) for wd in widths],
        compiler_params=_PARAMS,
        name="norm_proj",
    )(x, g.reshape(1, D_MODEL), w)


def _out_proj_body(x_ref, y_ref, w_ref, o_ref):
    o_ref[...] = x_ref[...] + jnp.dot(y_ref[...].astype(BF16), w_ref[...], preferred_element_type=F32)


def out_proj(x, y, w):
    k = y.shape[1]
    return pl.pallas_call(
        _out_proj_body,
        grid=(T_ALL // TOKEN_TILE,),
        in_specs=[_rows(D_MODEL), _rows(k), _resident((k, D_MODEL))],
        out_specs=_rows(D_MODEL),
        out_shape=jax.ShapeDtypeStruct((T_ALL, D_MODEL), F32),
        compiler_params=_PARAMS,
        name="out_proj",
    )(x, y, w)


def _mlp_body(x_ref, g_ref, wu_ref, wd_ref, gf_ref, o_ref, *, final):
    x = x_ref[...]
    xn = _rms(x, g_ref[...]).astype(BF16)
    acc = x
    for c in range(D_FF // FF_CHUNK):
        cols = slice(c * FF_CHUNK, (c + 1) * FF_CHUNK)
        h = jnp.maximum(jnp.dot(xn, wu_ref[:, cols], preferred_element_type=F32), 0.0)
        acc = acc + jnp.dot((h * h).astype(BF16), wd_ref[cols, :], preferred_element_type=F32)
    if final:
        acc = _rms(acc, gf_ref[...])
    o_ref[...] = acc


def mlp(x, g, w_up, w_down, g_final, final):
    return pl.pallas_call(
        functools.partial(_mlp_body, final=final),
        grid=(T_ALL // TOKEN_TILE,),
        in_specs=[_rows(D_MODEL), _resident((1, D_MODEL)), _resident((D_MODEL, D_FF)),
                  _resident((D_FF, D_MODEL)), _resident((1, D_MODEL))],
        out_specs=_rows(D_MODEL),
        out_shape=jax.ShapeDtypeStruct((T_ALL, D_MODEL), F32),
        compiler_params=_PARAMS,
        name="mlp",
    )(x, g.reshape(1, D_MODEL), w_up, w_down, g_final.reshape(1, D_MODEL))


def masked_softmax(s, mask):
    p = jax.nn.softmax(jnp.where(mask, s, NEG_INF), axis=-1)
    return jnp.where(mask, p, 0.0)


def rope(x, pos):
    half = ROPE_DIM // 2
    inv = ROPE_THETA ** (-jnp.arange(half, dtype=jnp.float32) / half)
    ang = pos.astype(jnp.float32)[:, None] * inv
    cos, sin = jnp.cos(ang)[:, None, :], jnp.sin(ang)[:, None, :]
    xr = x[..., :ROPE_DIM].astype(jnp.float32)
    x1, x2 = xr[..., :half], xr[..., half:]
    rot = jnp.concatenate([x1 * cos - x2 * sin, x2 * cos + x1 * sin], -1)
    return jnp.concatenate([rot.astype(x.dtype), x[..., ROPE_DIM:]], -1)


def _cmul(ar, ai, br, bi):
    return ar * br - ai * bi, ar * bi + ai * br


def _s5_combine(e1, e2):
    a1r, a1i, b1r, b1i = e1
    a2r, a2i, b2r, b2i = e2
    ar, ai = _cmul(a2r, a2i, a1r, a1i)
    br, bi = _cmul(a2r, a2i, b1r, b1i)
    return ar, ai, br + b2r, bi + b2i


def s5_mixer(u, h0, a_re, a_im, log_dt, b_re, b_im, c_re, c_im, d_skip, glu_w, glu_b):
    f32 = jnp.float32
    bt, s, _ = u.shape
    ug = u.astype(f32).reshape(bt, s, S5_GROUPS, S5_GROUP)
    dt = jnp.exp(log_dt.astype(f32))[:, None]
    mag = jnp.exp(a_re * dt)
    abar_re, abar_im = mag * jnp.cos(a_im * dt), mag * jnp.sin(a_im * dt)
    den = a_re * a_re + a_im * a_im
    f_re = ((abar_re - 1.0) * a_re + abar_im * a_im) / den
    f_im = (abar_im * a_re - (abar_re - 1.0) * a_im) / den
    bbar_re, bbar_im = _cmul(f_re[..., None], f_im[..., None], b_re, b_im)
    bu_re = jnp.einsum('bsgc,gpc->sbgp', ug, bbar_re)
    bu_im = jnp.einsum('bsgc,gpc->sbgp', ug, bbar_im)
    shape_a = (s, 1, S5_GROUPS, S5_STATE)
    acr, aci, hr, hi = lax.associative_scan(
        _s5_combine,
        (jnp.broadcast_to(abar_re, shape_a), jnp.broadcast_to(abar_im, shape_a), bu_re, bu_im),
        axis=0)
    dr, di = _cmul(acr, aci, h0[None, ..., 0], h0[None, ..., 1])
    hr = hr + dr
    hi = hi + di
    y = (jnp.einsum('sbgp,gcp->bsgc', hr, c_re) - jnp.einsum('sbgp,gcp->bsgc', hi, c_im))
    y = y + d_skip.reshape(S5_GROUPS, S5_GROUP) * ug
    zg = jnp.einsum('bsgc,gce->bsge', y, glu_w) + glu_b
    out = zg[..., :S5_GROUP] * jax.nn.sigmoid(zg[..., S5_GROUP:])
    h_last = jnp.stack([hr[-1], hi[-1]], -1)
    return out.reshape(bt, s, S5_WIDTH), h_last


def _segsum_exp(a):
    t = a.shape[-1]
    cs = jnp.cumsum(a, -1)
    tril = np.tril(np.ones((t, t), dtype=bool))
    return jnp.exp(jnp.where(tril, cs[..., :, None] - cs[..., None, :], -jnp.inf))


def _block(n, pref):
    return pref if n % pref == 0 else n


def ssd_mixer(z, xbc, dt_raw, conv_buf, h0, conv_w, conv_b, dt_bias, a_log, d_skip, norm_g):
    f32 = jnp.float32
    bt, s, _ = xbc.shape
    xpad = jnp.concatenate([conv_buf, xbc], 1)
    conv = conv_b + sum(xpad[:, k:k + s] * conv_w[k] for k in range(SSD_CONV))
    new_buf = xpad[:, s:]
    xbc_a = jax.nn.silu(conv)
    n_bc = SSD_GROUPS * SSD_STATE
    x = xbc_a[..., :SSD_INNER].reshape(bt, s, SSD_HEADS, SSD_HEAD_DIM)
    bm = xbc_a[..., SSD_INNER:SSD_INNER + n_bc]
    cm = xbc_a[..., SSD_INNER + n_bc:]
    dt = jax.nn.softplus(dt_raw + dt_bias)
    a = -jnp.exp(a_log)
    q = _block(s, SSD_CHUNK)
    nc = s // q
    r = SSD_HEADS // SSD_GROUPS
    xdt = (x * dt[..., None]).reshape(bt, nc, q, SSD_GROUPS, r, SSD_HEAD_DIM)
    bm = bm.reshape(bt, nc, q, SSD_GROUPS, SSD_STATE)
    cm = cm.reshape(bt, nc, q, SSD_GROUPS, SSD_STATE)
    a_dt = (dt * a).reshape(bt, nc, q, SSD_GROUPS, r).transpose(0, 3, 4, 1, 2)
    a_cs = jnp.cumsum(a_dt, -1)
    lmat = _segsum_exp(a_dt)
    cb = jnp.einsum('bclgn,bcsgn->bcgls', cm, bm)
    y_diag = jnp.einsum('bcgls,bgrcls,bcsgrp->bclgrp', cb, lmat, xdt)
    decay = jnp.exp(a_cs[..., -1:] - a_cs)
    states = jnp.einsum('bclgn,bgrcl,bclgrp->bcgrpn', bm, decay, xdt)
    h0g = h0.reshape(bt, 1, SSD_GROUPS, r, SSD_HEAD_DIM, SSD_STATE)
    states = jnp.concatenate([h0g, states], 1)
    chunk_decay = _segsum_exp(jnp.pad(a_cs[..., -1], [(0, 0)] * 3 + [(1, 0)]))
    states = jnp.einsum('bgrzc,bcgrpn->bzgrpn', chunk_decay, states)
    y_off = jnp.einsum('bclgn,bcgrpn,bgrcl->bclgrp', cm, states[:, :-1], jnp.exp(a_cs))
    y = (y_diag + y_off).reshape(bt, s, SSD_HEADS, SSD_HEAD_DIM) + x * d_skip[:, None]
    y = y.reshape(bt, s, SSD_INNER) * jax.nn.silu(z)
    y = _rms(y, norm_g)
    h_last = states[:, -1].reshape(bt, SSD_HEADS, SSD_HEAD_DIM, SSD_STATE)
    return y, h_last, new_buf


def compress_blocks(k, w1, w2, pe):
    bt, t = k.shape[:2]
    ratio = CMP_LEN // CMP_STRIDE
    n_chunk = t // CMP_STRIDE
    n_cmp = n_chunk - ratio + 1
    ch = k[:, :n_chunk * CMP_STRIDE].reshape(bt, n_chunk, CMP_STRIDE, N_KV, HEAD_DIM)
    blocks = jnp.concatenate([ch[:, j:j + n_cmp] for j in range(ratio)], axis=2)
    blocks = blocks + pe[:, None, :]
    flat = blocks.transpose(0, 1, 3, 2, 4).reshape(bt, n_cmp, N_KV, CMP_LEN * HEAD_DIM)
    return jax.nn.silu(flat @ w1) @ w2


def nsa_compressed(q, full_cmp, q_off, w1, w2, pe):
    s = q.shape[1]
    kc = compress_blocks(full_cmp[:, :, 0], w1[0], w2[0], pe[0])
    vc = compress_blocks(full_cmp[:, :, 1], w1[1], w2[1], pe[1])
    n_cmp = kc.shape[1]
    q_pos = q_off + np.arange(s)
    ends = np.arange(n_cmp) * CMP_STRIDE + CMP_LEN - 1
    mask = ends[None, :] <= q_pos[:, None]
    sc = jnp.einsum('bsgrd,bngd->bsgrn', q, kc).astype(jnp.float32) * ATT_SCALE
    p = masked_softmax(sc, mask[None, :, None, None, :])
    return jnp.einsum('bsgrn,bngd->bsgrd', p, vc), p


def nsa_selected(q, full_sel, p_cmp, q_off):
    bt, s = q.shape[:2]
    t = full_sel.shape[1]
    n_slc = -(-t // SEL_LEN)
    n_cmp = p_cmp.shape[-1]
    ci = np.arange(n_cmp)[:, None]
    sj = np.arange(n_slc)[None, :]
    overlap = ((ci * CMP_STRIDE < (sj + 1) * SEL_LEN)
               & (ci * CMP_STRIDE + CMP_LEN > sj * SEL_LEN)).astype(np.float32)
    imp = jnp.einsum('bsgn,nj->bsgj', p_cmp.sum(3), overlap)
    q_pos = q_off + np.arange(s)
    q_blk = q_pos // SEL_LEN
    jj = np.arange(n_slc)[None, :]
    valid = jj * SEL_LEN <= q_pos[:, None]
    forced = (jj == 0) | (jj == q_blk[:, None]) | (jj == q_blk[:, None] - 1)
    score = jnp.where(valid[None, :, None, :],
                      imp + np.where(forced, FORCE_BONUS, 0.0).astype(np.float32)[None, :, None, :],
                      NEG_INF)
    kk = min(SEL_TOPK, n_slc)
    _, idx = lax.top_k(score, kk)
    kv = jnp.pad(full_sel, ((0, 0), (0, n_slc * SEL_LEN - t), (0, 0), (0, 0), (0, 0)))
    kv = kv.reshape(bt, n_slc, SEL_LEN, 2, N_KV, HEAD_DIM).transpose(0, 4, 1, 2, 3, 5)
    kt, vt = kv[..., 0, :], kv[..., 1, :]
    qb = _block(s, SEL_QBLOCK)
    nb = s // qb
    q_blocks = q.reshape(bt, nb, qb, N_KV, HEADS_PER_KV, HEAD_DIM).swapaxes(0, 1)
    idx_blocks = idx.reshape(bt, nb, qb, N_KV, kk).swapaxes(0, 1)
    pos_blocks = jnp.asarray(q_pos.reshape(nb, qb))
    bi = jnp.arange(bt)[:, None, None, None]
    gi = jnp.arange(N_KV)[None, None, :, None]
    offs = jnp.arange(SEL_LEN)

    def one_block(args):
        qblk, iblk, pblk = args
        kg = kt[bi, gi, iblk]
        vg = vt[bi, gi, iblk]
        kpos = iblk[..., None] * SEL_LEN + offs
        mask = (kpos <= pblk[None, :, None, None, None])[:, :, :, None]
        sc = jnp.einsum('bqgrd,bqgkld->bqgrkl', qblk, kg).astype(jnp.float32) * ATT_SCALE
        shp = sc.shape
        p = masked_softmax(sc.reshape(shp[:4] + (kk * SEL_LEN,)),
                           jnp.broadcast_to(mask, shp).reshape(shp[:4] + (kk * SEL_LEN,))).reshape(shp)
        return jnp.einsum('bqgrkl,bqgkld->bqgrd', p, vg)

    o = lax.map(one_block, (q_blocks, idx_blocks, pos_blocks))
    return o.swapaxes(0, 1).reshape(bt, s, N_KV, HEADS_PER_KV, HEAD_DIM)


def nsa_window(q, full_win, n_prev):
    bt, s = q.shape[:2]
    qb = _block(s, ATT_QBLOCK)
    nb = s // qb
    band = qb + WINDOW - 1
    kv = jnp.pad(full_win, ((0, 0), (WINDOW - 1, 0), (0, 0), (0, 0), (0, 0)))
    q_blocks = q.reshape(bt, nb, qb, N_KV, HEADS_PER_KV, HEAD_DIM).swapaxes(0, 1)

    def one_block(args):
        blk, qblk = args
        start = blk * qb + n_prev
        kvb = lax.dynamic_slice_in_dim(kv, start, band, axis=1)
        k_idx = start - (WINDOW - 1) + jnp.arange(band)
        q_idx = start + jnp.arange(qb)
        rel = q_idx[:, None] - k_idx[None, :]
        mask = (k_idx[None, :] >= 0) & (rel >= 0) & (rel < WINDOW)
        sc = jnp.einsum('bqgrd,blgd->bqgrl', qblk, kvb[:, :, 0]).astype(jnp.float32) * ATT_SCALE
        p = masked_softmax(sc, mask[None, :, None, None, :])
        return jnp.einsum('bqgrl,blgd->bqgrd', p, kvb[:, :, 1])

    o = lax.map(one_block, (jnp.arange(nb), q_blocks))
    return o.swapaxes(0, 1).reshape(bt, s, N_KV, HEADS_PER_KV, HEAD_DIM)


def _heads(t, n):
    return t.reshape(t.shape[0], t.shape[1], n, HEAD_DIM)


def nsa_mix(q, kv, g, q_off, past_cmp, past_sel, win_buf, cmp_w1, cmp_w2, cmp_pos):
    bt, s, _ = q.shape
    k_cmp, v_cmp, k_sel, v_sel, k_win, v_win = jnp.split(kv, 6, axis=-1)
    pos = q_off + jnp.arange(s)
    q = rope(_heads(q, N_HEADS), pos).reshape(bt, s, N_KV, HEADS_PER_KV, HEAD_DIM)
    new_cmp = jnp.stack([rope(_heads(k_cmp, N_KV), pos), _heads(v_cmp, N_KV)], 2)
    new_sel = jnp.stack([rope(_heads(k_sel, N_KV), pos), _heads(v_sel, N_KV)], 2)
    new_win = jnp.stack([rope(_heads(k_win, N_KV), pos), _heads(v_win, N_KV)], 2)
    full_cmp = jnp.concatenate([past_cmp, new_cmp], 1)
    full_sel = jnp.concatenate([past_sel, new_sel], 1)
    full_win = jnp.concatenate([win_buf, new_win], 1)
    o_cmp, p_cmp = nsa_compressed(q, full_cmp, q_off, cmp_w1, cmp_w2, cmp_pos)
    o_sel = nsa_selected(q, full_sel, p_cmp, q_off)
    o_win = nsa_window(q, full_win, win_buf.shape[1])
    gate = jax.nn.sigmoid(g[..., :3 * N_HEADS]).reshape(bt, s, N_KV, HEADS_PER_KV, 3)
    o = gate[..., 0:1] * o_cmp + gate[..., 1:2] * o_sel + gate[..., 2:3] * o_win
    keep = min(WINDOW, full_win.shape[1])
    return o.reshape(bt, s, N_HEADS * HEAD_DIM), new_cmp, new_sel, full_win[:, full_win.shape[1] - keep:]


def _split_groups(a):
    w = a.shape[-1]
    return a[:T_PROMPT].reshape(BATCH, SEQ, w), a[T_PROMPT:].reshape(DEC_BATCH, DEC_SEQ, w)


def _join_groups(p, s):
    return jnp.concatenate([p.reshape(T_PROMPT, -1), s.reshape(T_SAMPLE, -1)], 0)


def _pad_cols(w, n):
    return jnp.pad(w, ((0, 0), (0, n - w.shape[1])))


def kernel(x_prompt, x_sample, state_s5, state_ssd, state_conv, cache_cmp_kv, cache_sel_kv, state_win_kv, page_table, norm_mix_even, w_in_even, s5_a_re, s5_a_im, s5_log_dt, s5_b_re, s5_b_im, s5_c_re, s5_c_im, s5_d, s5_glu_w, s5_glu_b, ssd_conv_w, ssd_conv_b, ssd_dt_bias, ssd_a_log, ssd_d, ssd_norm, w_out_even, norm_mix_odd, w_in_odd, cmp_w1, cmp_w2, cmp_pos, w_out_odd, norm_mlp, w_up, w_down, norm_final):
    h = _join_groups(x_prompt, x_sample)
    even_widths = (S5_WIDTH, SSD_INNER, SSD_CONV_DIM, _pad_lanes(SSD_HEADS))
    odd_widths = (N_HEADS * HEAD_DIM, 6 * KV_W, _pad_lanes(3 * N_HEADS))
    outs = {k: [] for k in ("s5_p", "s5_s", "ssd_p", "ssd_s", "conv_p", "conv_s",
                            "cmp_p", "cmp_s", "sel_p", "sel_s", "win_p", "win_s")}
    for layer in range(DEPTH):
        i = layer // 2
        if layer % 2 == 0:
            w_in = _pad_cols(w_in_even[i], sum(even_widths)).astype(BF16)
            u, z, xbc, dt_raw = norm_proj(h, norm_mix_even[i], w_in, even_widths)
            dt_raw = dt_raw[:, :SSD_HEADS]
            s5w = (s5_a_re[i], s5_a_im[i], s5_log_dt[i], s5_b_re[i], s5_b_im[i], s5_c_re[i], s5_c_im[i],
                   s5_d[i], s5_glu_w[i], s5_glu_b[i])
            ssdw = (ssd_conv_w[i], ssd_conv_b[i], ssd_dt_bias[i], ssd_a_log[i], ssd_d[i], ssd_norm[i])
            ys = []
            for grp, (ug, zg, xg, dg) in enumerate(zip(*map(_split_groups, (u, z, xbc, dt_raw)))):
                if grp == 0:
                    s5_h0 = jnp.zeros((BATCH, S5_GROUPS, S5_STATE, 2), F32)
                    ssd_h0 = jnp.zeros((BATCH, SSD_HEADS, SSD_HEAD_DIM, SSD_STATE), F32)
                    conv0 = jnp.zeros((BATCH, SSD_CONV - 1, SSD_CONV_DIM), F32)
                else:
                    s5_h0, ssd_h0, conv0 = state_s5[i], state_ssd[i], state_conv[i]
                ya, s5_h = s5_mixer(ug, s5_h0, *s5w)
                yb, ssd_h, conv_new = ssd_mixer(zg, xg, dg, conv0, ssd_h0, *ssdw)
                ys.append(jnp.concatenate([ya, yb], -1))
                sfx = "_p" if grp == 0 else "_s"
                outs["s5" + sfx].append(s5_h)
                outs["ssd" + sfx].append(ssd_h)
                outs["conv" + sfx].append(conv_new)
            h = out_proj(h, _join_groups(*ys), w_out_even[i].astype(BF16))
        else:
            w_in = _pad_cols(w_in_odd[i], sum(odd_widths)).astype(BF16)
            q, kv, g = norm_proj(h, norm_mix_odd[i], w_in, odd_widths)
            ys = []
            for grp, (qg, kvg, gg) in enumerate(zip(*map(_split_groups, (q, kv, g)))):
                if grp == 0:
                    empty = jnp.zeros((BATCH, 0, 2, N_KV, HEAD_DIM), F32)
                    q_off, past_cmp, past_sel, win_buf = 0, empty, empty, empty
                else:
                    q_off = PAST_LEN
                    past_cmp = cache_cmp_kv[i][page_table].reshape(DEC_BATCH, PAST_LEN, 2, N_KV, HEAD_DIM)
                    past_sel = cache_sel_kv[i][page_table].reshape(DEC_BATCH, PAST_LEN, 2, N_KV, HEAD_DIM)
                    win_buf = state_win_kv[i]
                y, a, b, c = nsa_mix(qg, kvg, gg, q_off, past_cmp, past_sel, win_buf,
                                     cmp_w1[i], cmp_w2[i], cmp_pos[i])
                ys.append(y)
                sfx = "_p" if grp == 0 else "_s"
                outs["cmp" + sfx].append(a)
                outs["sel" + sfx].append(b)
                outs["win" + sfx].append(c)
            h = out_proj(h, _join_groups(*ys), w_out_odd[i].astype(BF16))
        h = mlp(h, norm_mlp[layer], w_up[layer].astype(BF16), w_down[layer].astype(BF16),
                norm_final, final=(layer == DEPTH - 1))
    y_prompt, y_sample = _split_groups(h)
    st = {k: jnp.stack(v) for k, v in outs.items()}
    return (y_prompt, y_sample, st["s5_p"], st["s5_s"], st["ssd_p"], st["ssd_s"], st["conv_p"], st["conv_s"],
            st["cmp_p"], st["cmp_s"], st["sel_p"], st["sel_s"], st["win_p"], st["win_s"])
```

```python
import functools

import jax
import jax.numpy as jnp
import numpy as np
from jax import lax
from jax.experimental import pallas as pl
from jax.experimental.pallas import tpu as pltpu

D_MODEL = 1024
BATCH = 8
SEQ = 2048
DEPTH = 4
DEC_BATCH = 128
DEC_SEQ = 4
PAST_LEN = 2048
PAGE_SIZE = 128

N_SSM = (DEPTH + 1) // 2
N_ATT = DEPTH // 2
NORM_EPS = 1e-5
D_FF = 4 * D_MODEL
NEG_INF = -1e30

S5_WIDTH = D_MODEL // 2
S5_GROUP = 16
S5_GROUPS = S5_WIDTH // S5_GROUP
S5_STATE = 64

SSD_INNER = D_MODEL
SSD_HEAD_DIM = 64
SSD_HEADS = SSD_INNER // SSD_HEAD_DIM
SSD_STATE = 128
SSD_GROUPS = 4
SSD_CONV = 4
SSD_CONV_DIM = SSD_INNER + 2 * SSD_GROUPS * SSD_STATE
SSD_CHUNK = 128
MIX_EVEN = S5_WIDTH + SSD_INNER
IN_EVEN = S5_WIDTH + SSD_INNER + SSD_CONV_DIM + SSD_HEADS

N_HEADS = 16
HEAD_DIM = D_MODEL // N_HEADS
N_KV = 2
HEADS_PER_KV = N_HEADS // N_KV
KV_W = N_KV * HEAD_DIM
ROPE_DIM = HEAD_DIM // 4
ROPE_THETA = 500000.0
ATT_SCALE = HEAD_DIM ** -0.5
CMP_LEN = 32
CMP_STRIDE = 16
CMP_HIDDEN = 2 * HEAD_DIM
SEL_LEN = 64
SEL_TOPK = 16
FORCE_BONUS = 1e4
WINDOW = 512
ATT_QBLOCK = 128
SEL_QBLOCK = 64
IN_ODD = N_HEADS * HEAD_DIM + 6 * KV_W + 3 * N_HEADS

T_PROMPT = BATCH * SEQ
T_SAMPLE = DEC_BATCH * DEC_SEQ
T_ALL = T_PROMPT + T_SAMPLE

LANE = 128
SUBLANE = 8
TOKEN_TILE = 512
FF_CHUNK = 1024
VMEM_LIMIT = 56 * 1024 * 1024

S5_SLAB_GROUPS = LANE // S5_GROUP
S5_SLABS = S5_GROUPS // S5_SLAB_GROUPS
S5_SLAB_STATE = S5_SLAB_GROUPS * S5_STATE
S5_PROMPT_STEPS = 256
SAMPLE_GROUPS = DEC_BATCH // SUBLANE

NSA_Q_TILE = 128
SEL_K_TILE = 256
WIN_K_TILE = 128
N_SLC_PROMPT = SEQ // SEL_LEN
N_CMP_PROMPT = SEQ // CMP_STRIDE - CMP_LEN // CMP_STRIDE + 1

F32 = jnp.float32
BF16 = jnp.bfloat16


def _pad_lanes(n):
    return -(-n // LANE) * LANE


def _rms(x, g):
    return x * lax.rsqrt(jnp.mean(x * x, -1, keepdims=True) + NORM_EPS) * g


def _resident(shape):
    return pl.BlockSpec(shape, lambda *_: (0,) * len(shape), pipeline_mode=pl.Buffered(1))


def _rows(width):
    return pl.BlockSpec((TOKEN_TILE, width), lambda i: (i, 0))


def _dot(a, b):
    return jnp.dot(a, b, preferred_element_type=F32)


def _dot_nt(a, b):
    return lax.dot_general(a, b, (((1,), (1,)), ((), ())), preferred_element_type=F32)


def _dot_tn(a, b):
    return lax.dot_general(a, b, (((0,), (0,)), ((), ())), preferred_element_type=F32)


def _split_bf16(x):
    hi = x.astype(BF16)
    return hi, (x - hi.astype(F32)).astype(BF16)


_PARAMS = pltpu.CompilerParams(dimension_semantics=("arbitrary",), vmem_limit_bytes=VMEM_LIMIT)
_PARAMS2 = pltpu.CompilerParams(dimension_semantics=("arbitrary", "arbitrary"), vmem_limit_bytes=VMEM_LIMIT)


def _rope_block(x, cos_f, sin_a, sin_b):
    return x * cos_f + pltpu.roll(x, ROPE_DIM // 2, 1) * sin_a + pltpu.roll(x, LANE - ROPE_DIM // 2, 1) * sin_b


def _norm_proj_body(x_ref, g_ref, w_ref, *refs, widths, rope_blocks):
    if rope_blocks is None:
        o_refs = refs
    else:
        cos_ref, sa_ref, sb_ref = refs[:3]
        o_refs = refs[3:]
    xn = _rms(x_ref[...], g_ref[...]).astype(BF16)
    off = 0
    for idx, (o_ref, wd) in enumerate(zip(o_refs, widths)):
        o_ref[...] = _dot(xn, w_ref[:, off:off + wd])
        off += wd
        if rope_blocks is not None:
            for blk in rope_blocks[idx]:
                cols = slice(blk * LANE, (blk + 1) * LANE)
                o_ref[:, cols] = _rope_block(o_ref[:, cols], cos_ref[...], sa_ref[...], sb_ref[...])


def norm_proj(x, g, w, widths, rope_tabs=None, rope_blocks=None):
    n = sum(widths)
    in_specs = [_rows(D_MODEL), _resident((1, D_MODEL)), _resident((D_MODEL, n))]
    args = [x, g.reshape(1, D_MODEL), w]
    if rope_blocks is not None:
        in_specs += [_rows(LANE)] * 3
        args += list(rope_tabs)
    return pl.pallas_call(
        functools.partial(_norm_proj_body, widths=widths, rope_blocks=rope_blocks),
        grid=(T_ALL // TOKEN_TILE,),
        in_specs=in_specs,
        out_specs=[_rows(wd) for wd in widths],
        out_shape=[jax.ShapeDtypeStruct((T_ALL, wd), F32) for wd in widths],
        compiler_params=_PARAMS,
        name="norm_proj",
    )(*args)


def _out_proj_body(x_ref, y_ref, w_ref, o_ref):
    o_ref[...] = x_ref[...] + _dot(y_ref[...].astype(BF16), w_ref[...])


def out_proj(x, y, w):
    k = y.shape[1]
    return pl.pallas_call(
        _out_proj_body,
        grid=(T_ALL // TOKEN_TILE,),
        in_specs=[_rows(D_MODEL), _rows(k), _resident((k, D_MODEL))],
        out_specs=_rows(D_MODEL),
        out_shape=jax.ShapeDtypeStruct((T_ALL, D_MODEL), F32),
        compiler_params=_PARAMS,
        name="out_proj",
    )(x, y, w)


def _mlp_body(x_ref, g_ref, wu_ref, wd_ref, gf_ref, o_ref, *, final):
    x = x_ref[...]
    xn = _rms(x, g_ref[...]).astype(BF16)
    acc = x
    for c in range(D_FF // FF_CHUNK):
        cols = slice(c * FF_CHUNK, (c + 1) * FF_CHUNK)
        h = jnp.maximum(_dot(xn, wu_ref[:, cols]), 0.0)
        acc = acc + _dot((h * h).astype(BF16), wd_ref[cols, :])
    if final:
        acc = _rms(acc, gf_ref[...])
    o_ref[...] = acc


def mlp(x, g, w_up, w_down, g_final, final):
    return pl.pallas_call(
        functools.partial(_mlp_body, final=final),
        grid=(T_ALL // TOKEN_TILE,),
        in_specs=[_rows(D_MODEL), _resident((1, D_MODEL)), _resident((D_MODEL, D_FF)),
                  _resident((D_FF, D_MODEL)), _resident((1, D_MODEL))],
        out_specs=_rows(D_MODEL),
        out_shape=jax.ShapeDtypeStruct((T_ALL, D_MODEL), F32),
        compiler_params=_PARAMS,
        name="mlp",
    )(x, g.reshape(1, D_MODEL), w_up, w_down, g_final.reshape(1, D_MODEL))


def _cmul(ar, ai, br, bi):
    return ar * br - ai * bi, ar * bi + ai * br


def s5_tables(a_re, a_im, log_dt, b_re, b_im, c_re, c_im, d_skip, glu_w, glu_b):
    dt = jnp.exp(log_dt)[:, None]
    mag = jnp.exp(a_re * dt)
    abar_re, abar_im = mag * jnp.cos(a_im * dt), mag * jnp.sin(a_im * dt)
    den = a_re * a_re + a_im * a_im
    f_re = ((abar_re - 1.0) * a_re + abar_im * a_im) / den
    f_im = (abar_im * a_re - (abar_re - 1.0) * a_im) / den
    bbar_re, bbar_im = _cmul(f_re[..., None], f_im[..., None], b_re, b_im)
    sg, ns = S5_SLAB_GROUPS, S5_SLABS
    eye = jnp.eye(sg, dtype=F32)
    a_tab = jnp.stack([abar_re.reshape(ns, S5_SLAB_STATE), abar_im.reshape(ns, S5_SLAB_STATE)], 1)

    def in_blockdiag(bb):
        return jnp.einsum('jgpc,gh->jgchp', bb.reshape(ns, sg, S5_STATE, S5_GROUP), eye).reshape(ns, LANE, S5_SLAB_STATE)

    def out_blockdiag(cc):
        return jnp.einsum('jgcp,gh->jhpgc', cc.reshape(ns, sg, S5_GROUP, S5_STATE), eye).reshape(ns, S5_SLAB_STATE, LANE)

    b_bd = jnp.concatenate([in_blockdiag(bbar_re), in_blockdiag(bbar_im)], -1).astype(BF16)
    c_bd = jnp.concatenate([out_blockdiag(c_re), out_blockdiag(-c_im)], 1).astype(BF16)
    g_bd = jnp.einsum('jgcke,gh->jgckhe', glu_w.reshape(ns, sg, S5_GROUP, 2, S5_GROUP), eye)
    g_bd = g_bd.reshape(ns, LANE, 2 * LANE).astype(BF16)
    g_b = glu_b.reshape(ns, sg, 2, S5_GROUP).transpose(0, 2, 1, 3).reshape(ns, 1, 2 * LANE)
    return a_tab, b_bd, c_bd, d_skip.reshape(ns, 1, LANE), g_bd, g_b


def _s5_body(u_ref, h0_ref, a_ref, b_ref, c_ref, d_ref, gw_ref, gb_ref, o_ref, hl_ref, st_ref, h_ref, *,
             n_seq, t_len):
    w = S5_SLAB_STATE

    @pl.when(pl.program_id(1) == 0)
    def _():
        h_ref[...] = h0_ref[...]

    ub = u_ref[...]
    st_ref[...] = _dot(ub.astype(BF16), b_ref[0])
    ar = jnp.broadcast_to(a_ref[0, 0:1, :], (SUBLANE, w))
    ai = jnp.broadcast_to(a_ref[0, 1:2, :], (SUBLANE, w))
    for s in range(n_seq):
        base = s * t_len * SUBLANE
        rows = slice(s * SUBLANE, (s + 1) * SUBLANE)

        def step(t, carry, base=base):
            hr, hi = carry
            r = pl.ds(pl.multiple_of(base + t * SUBLANE, SUBLANE), SUBLANE)
            nhr = ar * hr - ai * hi + st_ref[r, 0:w]
            nhi = ar * hi + ai * hr + st_ref[r, w:2 * w]
            st_ref[r, 0:w] = nhr
            st_ref[r, w:2 * w] = nhi
            return nhr, nhi

        hr, hi = lax.fori_loop(0, t_len, step, (h_ref[rows, 0:w], h_ref[rows, w:2 * w]), unroll=min(t_len, 8))
        h_ref[rows, 0:w] = hr
        h_ref[rows, w:2 * w] = hi
    y = _dot(st_ref[...].astype(BF16), c_ref[0]) + d_ref[0] * ub
    zg = _dot(y.astype(BF16), gw_ref[0]) + gb_ref[0]
    o_ref[...] = zg[:, :LANE] * jax.nn.sigmoid(zg[:, LANE:])
    hl_ref[...] = h_ref[...]


def s5_scan(u, h0, tabs, *, n_seq, t_len):
    blk = n_seq * t_len * SUBLANE
    n_chunks = u.shape[0] // blk
    assert n_chunks * blk == u.shape[0] and (n_seq == 1 or n_chunks == 1)
    hb = n_seq * SUBLANE
    slab = lambda shape: pl.BlockSpec((1,) + shape, lambda j, c: (j, 0, 0))
    return pl.pallas_call(
        functools.partial(_s5_body, n_seq=n_seq, t_len=t_len),
        grid=(S5_SLABS, n_chunks),
        in_specs=[pl.BlockSpec((blk, LANE), lambda j, c: (c, j)),
                  pl.BlockSpec((hb, 2 * S5_SLAB_STATE), lambda j, c: (0, j)),
                  slab((2, S5_SLAB_STATE)), slab((LANE, 2 * S5_SLAB_STATE)), slab((2 * S5_SLAB_STATE, LANE)),
                  slab((1, LANE)), slab((LANE, 2 * LANE)), slab((1, 2 * LANE))],
        out_specs=[pl.BlockSpec((blk, LANE), lambda j, c: (c, j)),
                   pl.BlockSpec((hb, 2 * S5_SLAB_STATE), lambda j, c: (0, j))],
        out_shape=[jax.ShapeDtypeStruct(u.shape, F32), jax.ShapeDtypeStruct(h0.shape, F32)],
        scratch_shapes=[pltpu.VMEM((blk, 2 * S5_SLAB_STATE), F32), pltpu.VMEM((hb, 2 * S5_SLAB_STATE), F32)],
        compiler_params=_PARAMS2,
        name="s5_scan",
    )(u, h0, *tabs)


def _s5_state_to_lanes(h):
    bt = h.shape[0]
    return h.reshape(bt, S5_SLABS, S5_SLAB_STATE, 2).transpose(0, 1, 3, 2).reshape(bt, -1)


def _s5_state_from_lanes(h):
    bt = h.shape[0]
    return h.reshape(bt, S5_SLABS, 2, S5_SLAB_STATE).transpose(0, 1, 3, 2).reshape(bt, S5_GROUPS, S5_STATE, 2)


def _nsa_consts():
    nq = N_SLC_PROMPT
    ci = np.arange(LANE)[None, :]
    sj = np.arange(nq)[:, None]
    overlap_t = ((ci * CMP_STRIDE < (sj + 1) * SEL_LEN) & (ci * CMP_STRIDE + CMP_LEN > sj * SEL_LEN)
                 & (ci < N_CMP_PROMPT)).astype(np.float32)
    keys = np.arange(SEQ)
    expand = (keys[None, :] // SEL_LEN == np.arange(nq)[:, None]).astype(np.float32)
    expand = expand.reshape(nq, SEQ // SEL_K_TILE, SEL_K_TILE).transpose(1, 0, 2)
    gate = np.zeros((3, LANE, N_HEADS * HEAD_DIM), np.float32)
    for br in range(3):
        for h in range(N_HEADS):
            gate[br, h * 3 + br, h * HEAD_DIM:(h + 1) * HEAD_DIM] = 1.0
    return jnp.asarray(overlap_t, BF16), jnp.asarray(expand, BF16), jnp.asarray(gate, BF16)


def _lane_halves(shape):
    return lax.broadcasted_iota(jnp.int32, shape, len(shape) - 1) < HEAD_DIM


def _kv_variants(k, v):
    low = _lane_halves(k.shape)
    k = k * ATT_SCALE
    k_sw = pltpu.roll(k, HEAD_DIM, 1)
    v_sw = pltpu.roll(v, HEAD_DIM, 1)
    kl = (jnp.where(low, k, 0.0), jnp.where(low, k_sw, 0.0))
    kr = (jnp.where(low, 0.0, k_sw), jnp.where(low, 0.0, k))
    v2 = (jnp.where(low, v, v_sw), jnp.where(low, v_sw, v))
    return [[a.astype(BF16) for a in t] for t in (kl, kr, v2)]


def _nsa_prompt_body(q_ref, g_ref, kv_ref, kc_ref, vc_ref, ov_ref, ex_ref, eg_ref, o_ref,
                     kl_ref, kr_ref, v2_ref, acc_ref, m_ref, l_ref, ob_ref):
    tq = NSA_Q_TILE
    qi = pl.program_id(1)
    q0 = qi * tq

    @pl.when(qi == 0)
    def _():
        for br in range(2):
            c0 = (2 + 2 * br) * LANE
            kl, kr, v2 = _kv_variants(kv_ref[:, c0:c0 + LANE], kv_ref[:, c0 + LANE:c0 + 2 * LANE])
            for g in range(N_KV):
                kl_ref[br, g] = kl[g]
                kr_ref[br, g] = kr[g]
                v2_ref[br, g] = v2[g]

    low = _lane_halves((tq, LANE))
    qpos = q0 + lax.broadcasted_iota(jnp.int32, (tq, 1), 0)
    kcl, kcr, vc2 = _kv_variants(kc_ref[0], vc_ref[0])
    n_iota = lax.broadcasted_iota(jnp.int32, (tq, LANE), 1)
    cmp_mask = (n_iota * CMP_STRIDE + (CMP_LEN - 1) <= qpos) & (n_iota < N_CMP_PROMPT)

    def pair_q(g, pr):
        c0 = (g * (HEADS_PER_KV // 2) + pr) * LANE
        return q_ref[:, c0:c0 + LANE].astype(BF16)

    def flash(br, g, n_tiles, tile0, tk, mask_fn, reverse):
        m_ref[...] = jnp.full(m_ref.shape, NEG_INF, F32)
        l_ref[...] = jnp.zeros(l_ref.shape, F32)
        acc_ref[...] = jnp.zeros(acc_ref.shape, F32)

        def tile(kt, carry):
            if reverse:
                kt = n_tiles - 1 - kt
            k0 = pl.multiple_of(tile0 + kt * tk, tk)
            mask = mask_fn(kt, k0)
            v_t = v2_ref[br, g, pl.ds(k0, tk), :]
            for pr in range(HEADS_PER_KV // 2):
                qp = pair_q(g, pr)
                pv = []
                alpha = []
                for side, k_ref in enumerate((kl_ref, kr_ref)):
                    hh = 2 * pr + side
                    s = jnp.where(mask, _dot_nt(qp, k_ref[br, g, pl.ds(k0, tk), :]), NEG_INF)
                    m_old = m_ref[hh]
                    m_new = jnp.maximum(m_old, jnp.max(s, -1, keepdims=True))
                    p = jnp.exp(s - m_new)
                    a = jnp.exp(m_old - m_new)
                    l_ref[hh] = a * l_ref[hh] + jnp.sum(p, -1, keepdims=True)
                    m_ref[hh] = m_new
                    pv.append(_dot(p.astype(BF16), v_t))
                    alpha.append(a)
                acc_ref[pr] = acc_ref[pr] * jnp.where(low, alpha[0], alpha[1]) + jnp.where(low, pv[0], pv[1])
            return carry

        lax.fori_loop(0, n_tiles, tile, 0)
        for pr in range(HEADS_PER_KV // 2):
            c0 = (g * (HEADS_PER_KV // 2) + pr) * LANE
            ob_ref[br + 1, :, c0:c0 + LANE] = acc_ref[pr] / jnp.where(low, l_ref[2 * pr], l_ref[2 * pr + 1])

    for g in range(N_KV):
        psum = jnp.zeros((tq, LANE), F32)
        for pr in range(HEADS_PER_KV // 2):
            qp = pair_q(g, pr)
            outs = []
            for kc in (kcl[g], kcr[g]):
                s = jnp.where(cmp_mask, _dot_nt(qp, kc), NEG_INF)
                p = jnp.where(cmp_mask, jnp.exp(s - jnp.max(s, -1, keepdims=True)), 0.0)
                p = p / jnp.maximum(jnp.sum(p, -1, keepdims=True), 1e-30)
                psum = psum + p
                outs.append(_dot(p.astype(BF16), vc2[g]))
            c0 = (g * (HEADS_PER_KV // 2) + pr) * LANE
            ob_ref[0, :, c0:c0 + LANE] = jnp.where(low, outs[0], outs[1])

        p_hi, p_lo = _split_bf16(psum)
        imp = _dot_nt(ov_ref[...], p_hi) + _dot_nt(ov_ref[...], p_lo)
        shp = (N_SLC_PROMPT, tq)
        jj = lax.broadcasted_iota(jnp.int32, shp, 0)
        qp_t = q0 + lax.broadcasted_iota(jnp.int32, shp, 1)
        qb_t = qp_t // SEL_LEN
        forced = (jj == 0) | (jj == qb_t) | (jj == qb_t - 1)
        score = jnp.where(jj * SEL_LEN <= qp_t, imp + jnp.where(forced, FORCE_BONUS, 0.0), NEG_INF)
        rank = jnp.zeros(shp, F32)
        for i in range(N_SLC_PROMPT):
            row = score[i:i + 1, :]
            ahead = (row > score) | ((row == score) & (jj > i))
            rank = rank + jnp.where(ahead, 1.0, 0.0)
        sel_t = jnp.where(rank < SEL_TOPK, 1.0, 0.0).astype(BF16)

        def sel_mask(kt, k0, sel_t=sel_t):
            chosen = _dot_tn(sel_t, ex_ref[kt]) > 0.5
            kpos = k0 + lax.broadcasted_iota(jnp.int32, (tq, SEL_K_TILE), 1)
            return chosen & (kpos <= qpos)

        flash(0, g, (q0 + tq + SEL_K_TILE - 1) // SEL_K_TILE, 0, SEL_K_TILE, sel_mask, reverse=False)

        def win_mask(kt, k0):
            kpos = k0 + lax.broadcasted_iota(jnp.int32, (tq, WIN_K_TILE), 1)
            return (kpos <= qpos) & (qpos - kpos < WINDOW)

        w0 = jnp.maximum(q0 - WINDOW, 0)
        flash(1, g, (q0 + tq - w0) // WIN_K_TILE, w0, WIN_K_TILE, win_mask, reverse=True)

    g_hi, g_lo = _split_bf16(jax.nn.sigmoid(g_ref[...]))
    out = jnp.zeros((tq, N_HEADS * HEAD_DIM), F32)
    for br in range(3):
        out = out + (_dot(g_hi, eg_ref[br]) + _dot(g_lo, eg_ref[br])) * ob_ref[br]
    o_ref[...] = out


def nsa_prompt(q, gate, kv, kc, vc):
    tq = NSA_Q_TILE
    qw = N_HEADS * HEAD_DIM
    ov, ex, eg = _nsa_consts()
    return pl.pallas_call(
        _nsa_prompt_body,
        grid=(kc.shape[0], SEQ // tq),
        in_specs=[pl.BlockSpec((tq, qw), lambda b, i: (i, b)),
                  pl.BlockSpec((tq, LANE), lambda b, i: (i, b)),
                  pl.BlockSpec((SEQ, 6 * KV_W), lambda b, i: (0, b)),
                  pl.BlockSpec((1, LANE, LANE), lambda b, i: (b, 0, 0)),
                  pl.BlockSpec((1, LANE, LANE), lambda b, i: (b, 0, 0)),
                  _resident(ov.shape), _resident(ex.shape), _resident(eg.shape)],
        out_specs=pl.BlockSpec((tq, qw), lambda b, i: (i, b)),
        out_shape=jax.ShapeDtypeStruct((SEQ, kc.shape[0] * qw), F32),
        scratch_shapes=[pltpu.VMEM((2, N_KV, SEQ, LANE), BF16), pltpu.VMEM((2, N_KV, SEQ, LANE), BF16),
                        pltpu.VMEM((2, N_KV, SEQ, LANE), BF16),
                        pltpu.VMEM((HEADS_PER_KV // 2, tq, LANE), F32),
                        pltpu.VMEM((HEADS_PER_KV, tq, 1), F32), pltpu.VMEM((HEADS_PER_KV, tq, 1), F32),
                        pltpu.VMEM((3, tq, qw), F32)],
        compiler_params=_PARAMS2,
        name="nsa_prompt",
    )(q, gate, kv, kc, vc, ov, ex, eg)


def masked_softmax(s, mask):
    p = jax.nn.softmax(jnp.where(mask, s, NEG_INF), axis=-1)
    return jnp.where(mask, p, 0.0)


def _segsum_exp(a):
    t = a.shape[-1]
    cs = jnp.cumsum(a, -1)
    tril = np.tril(np.ones((t, t), dtype=bool))
    return jnp.exp(jnp.where(tril, cs[..., :, None] - cs[..., None, :], -jnp.inf))


def _block(n, pref):
    return pref if n % pref == 0 else n


def ssd_mixer(z, xbc, dt_raw, conv_buf, h0, conv_w, conv_b, dt_bias, a_log, d_skip, norm_g):
    bt, s, _ = xbc.shape
    xpad = jnp.concatenate([conv_buf, xbc], 1)
    conv = conv_b + sum(xpad[:, k:k + s] * conv_w[k] for k in range(SSD_CONV))
    new_buf = xpad[:, s:]
    xbc_a = jax.nn.silu(conv)
    n_bc = SSD_GROUPS * SSD_STATE
    x = xbc_a[..., :SSD_INNER].reshape(bt, s, SSD_HEADS, SSD_HEAD_DIM)
    bm = xbc_a[..., SSD_INNER:SSD_INNER + n_bc]
    cm = xbc_a[..., SSD_INNER + n_bc:]
    dt = jax.nn.softplus(dt_raw + dt_bias)
    a = -jnp.exp(a_log)
    q = _block(s, SSD_CHUNK)
    nc = s // q
    r = SSD_HEADS // SSD_GROUPS
    xdt = (x * dt[..., None]).reshape(bt, nc, q, SSD_GROUPS, r, SSD_HEAD_DIM)
    bm = bm.reshape(bt, nc, q, SSD_GROUPS, SSD_STATE)
    cm = cm.reshape(bt, nc, q, SSD_GROUPS, SSD_STATE)
    a_dt = (dt * a).reshape(bt, nc, q, SSD_GROUPS, r).transpose(0, 3, 4, 1, 2)
    a_cs = jnp.cumsum(a_dt, -1)
    lmat = _segsum_exp(a_dt)
    cb = jnp.einsum('bclgn,bcsgn->bcgls', cm, bm)
    y_diag = jnp.einsum('bcgls,bgrcls,bcsgrp->bclgrp', cb, lmat, xdt)
    decay = jnp.exp(a_cs[..., -1:] - a_cs)
    states = jnp.einsum('bclgn,bgrcl,bclgrp->bcgrpn', bm, decay, xdt)
    h0g = h0.reshape(bt, 1, SSD_GROUPS, r, SSD_HEAD_DIM, SSD_STATE)
    states = jnp.concatenate([h0g, states], 1)
    chunk_decay = _segsum_exp(jnp.pad(a_cs[..., -1], [(0, 0)] * 3 + [(1, 0)]))
    states = jnp.einsum('bgrzc,bcgrpn->bzgrpn', chunk_decay, states)
    y_off = jnp.einsum('bclgn,bcgrpn,bgrcl->bclgrp', cm, states[:, :-1], jnp.exp(a_cs))
    y = (y_diag + y_off).reshape(bt, s, SSD_HEADS, SSD_HEAD_DIM) + x * d_skip[:, None]
    y = y.reshape(bt, s, SSD_INNER) * jax.nn.silu(z)
    y = _rms(y, norm_g)
    h_last = states[:, -1].reshape(bt, SSD_HEADS, SSD_HEAD_DIM, SSD_STATE)
    return y, h_last, new_buf


def compress_blocks(k, w1, w2, pe):
    bt, t = k.shape[:2]
    ratio = CMP_LEN // CMP_STRIDE
    n_chunk = t // CMP_STRIDE
    n_cmp = n_chunk - ratio + 1
    ch = k[:, :n_chunk * CMP_STRIDE].reshape(bt, n_chunk, CMP_STRIDE, N_KV, HEAD_DIM)
    blocks = jnp.concatenate([ch[:, j:j + n_cmp] for j in range(ratio)], axis=2)
    blocks = blocks + pe[:, None, :]
    flat = blocks.transpose(0, 1, 3, 2, 4).reshape(bt, n_cmp, N_KV, CMP_LEN * HEAD_DIM)
    return jax.nn.silu(flat @ w1) @ w2


def nsa_compressed(q, full_cmp, q_off, w1, w2, pe):
    s = q.shape[1]
    kc = compress_blocks(full_cmp[:, :, 0], w1[0], w2[0], pe[0])
    vc = compress_blocks(full_cmp[:, :, 1], w1[1], w2[1], pe[1])
    n_cmp = kc.shape[1]
    q_pos = q_off + np.arange(s)
    ends = np.arange(n_cmp) * CMP_STRIDE + CMP_LEN - 1
    mask = ends[None, :] <= q_pos[:, None]
    sc = jnp.einsum('bsgrd,bngd->bsgrn', q, kc).astype(jnp.float32) * ATT_SCALE
    p = masked_softmax(sc, mask[None, :, None, None, :])
    return jnp.einsum('bsgrn,bngd->bsgrd', p, vc), p


def nsa_selected(q, full_sel, p_cmp, q_off):
    bt, s = q.shape[:2]
    t = full_sel.shape[1]
    n_slc = -(-t // SEL_LEN)
    n_cmp = p_cmp.shape[-1]
    ci = np.arange(n_cmp)[:, None]
    sj = np.arange(n_slc)[None, :]
    overlap = ((ci * CMP_STRIDE < (sj + 1) * SEL_LEN)
               & (ci * CMP_STRIDE + CMP_LEN > sj * SEL_LEN)).astype(np.float32)
    imp = jnp.einsum('bsgn,nj->bsgj', p_cmp.sum(3), overlap)
    q_pos = q_off + np.arange(s)
    q_blk = q_pos // SEL_LEN
    jj = np.arange(n_slc)[None, :]
    valid = jj * SEL_LEN <= q_pos[:, None]
    forced = (jj == 0) | (jj == q_blk[:, None]) | (jj == q_blk[:, None] - 1)
    score = jnp.where(valid[None, :, None, :],
                      imp + np.where(forced, FORCE_BONUS, 0.0).astype(np.float32)[None, :, None, :],
                      NEG_INF)
    kk = min(SEL_TOPK, n_slc)
    _, idx = lax.top_k(score, kk)
    kv = jnp.pad(full_sel, ((0, 0), (0, n_slc * SEL_LEN - t), (0, 0), (0, 0), (0, 0)))
    kv = kv.reshape(bt, n_slc, SEL_LEN, 2, N_KV, HEAD_DIM).transpose(0, 4, 1, 2, 3, 5)
    kt, vt = kv[..., 0, :], kv[..., 1, :]
    qb = _block(s, SEL_QBLOCK)
    nb = s // qb
    q_blocks = q.reshape(bt, nb, qb, N_KV, HEADS_PER_KV, HEAD_DIM).swapaxes(0, 1)
    idx_blocks = idx.reshape(bt, nb, qb, N_KV, kk).swapaxes(0, 1)
    pos_blocks = jnp.asarray(q_pos.reshape(nb, qb))
    bi = jnp.arange(bt)[:, None, None, None]
    gi = jnp.arange(N_KV)[None, None, :, None]
    offs = jnp.arange(SEL_LEN)

    def one_block(args):
        qblk, iblk, pblk = args
        kg = kt[bi, gi, iblk]
        vg = vt[bi, gi, iblk]
        kpos = iblk[..., None] * SEL_LEN + offs
        mask = (kpos <= pblk[None, :, None, None, None])[:, :, :, None]
        sc = jnp.einsum('bqgrd,bqgkld->bqgrkl', qblk, kg).astype(jnp.float32) * ATT_SCALE
        shp = sc.shape
        p = masked_softmax(sc.reshape(shp[:4] + (kk * SEL_LEN,)),
                           jnp.broadcast_to(mask, shp).reshape(shp[:4] + (kk * SEL_LEN,))).reshape(shp)
        return jnp.einsum('bqgrkl,bqgkld->bqgrd', p, vg)

    o = lax.map(one_block, (q_blocks, idx_blocks, pos_blocks))
    return o.swapaxes(0, 1).reshape(bt, s, N_KV, HEADS_PER_KV, HEAD_DIM)


def nsa_window(q, full_win, n_prev):
    bt, s = q.shape[:2]
    qb = _block(s, ATT_QBLOCK)
    nb = s // qb
    band = qb + WINDOW - 1
    kv = jnp.pad(full_win, ((0, 0), (WINDOW - 1, 0), (0, 0), (0, 0), (0, 0)))
    q_blocks = q.reshape(bt, nb, qb, N_KV, HEADS_PER_KV, HEAD_DIM).swapaxes(0, 1)

    def one_block(args):
        blk, qblk = args
        start = blk * qb + n_prev
        kvb = lax.dynamic_slice_in_dim(kv, start, band, axis=1)
        k_idx = start - (WINDOW - 1) + jnp.arange(band)
        q_idx = start + jnp.arange(qb)
        rel = q_idx[:, None] - k_idx[None, :]
        mask = (k_idx[None, :] >= 0) & (rel >= 0) & (rel < WINDOW)
        sc = jnp.einsum('bqgrd,blgd->bqgrl', qblk, kvb[:, :, 0]).astype(jnp.float32) * ATT_SCALE
        p = masked_softmax(sc, mask[None, :, None, None, :])
        return jnp.einsum('bqgrl,blgd->bqgrd', p, kvb[:, :, 1])

    o = lax.map(one_block, (jnp.arange(nb), q_blocks))
    return o.swapaxes(0, 1).reshape(bt, s, N_KV, HEADS_PER_KV, HEAD_DIM)


def _kv_rows(kv, branch):
    bt, s, _ = kv.shape
    return kv[..., 2 * branch * KV_W:(2 * branch + 2) * KV_W].reshape(bt, s, 2, N_KV, HEAD_DIM)


def nsa_sample_mix(q, kv, g, past_cmp, past_sel, win_buf, cmp_w1, cmp_w2, cmp_pos):
    bt, s, _ = q.shape
    q = q.reshape(bt, s, N_KV, HEADS_PER_KV, HEAD_DIM)
    new_cmp, new_sel, new_win = (_kv_rows(kv, br) for br in range(3))
    full_cmp = jnp.concatenate([past_cmp, new_cmp], 1)
    full_sel = jnp.concatenate([past_sel, new_sel], 1)
    full_win = jnp.concatenate([win_buf, new_win], 1)
    o_cmp, p_cmp = nsa_compressed(q, full_cmp, PAST_LEN, cmp_w1, cmp_w2, cmp_pos)
    o_sel = nsa_selected(q, full_sel, p_cmp, PAST_LEN)
    o_win = nsa_window(q, full_win, win_buf.shape[1])
    gate = jax.nn.sigmoid(g[..., :3 * N_HEADS]).reshape(bt, s, N_KV, HEADS_PER_KV, 3)
    o = gate[..., 0:1] * o_cmp + gate[..., 1:2] * o_sel + gate[..., 2:3] * o_win
    keep = min(WINDOW, full_win.shape[1])
    return o.reshape(bt, s, N_HEADS * HEAD_DIM), new_cmp, new_sel, full_win[:, full_win.shape[1] - keep:]


def _prompt_to_bsw(a):
    return a.reshape(SEQ, BATCH, -1).transpose(1, 0, 2)


def _prompt_from_bsw(a):
    return a.transpose(1, 0, 2).reshape(T_PROMPT, -1)


def _sample_to_bsw(a):
    return a.reshape(SAMPLE_GROUPS, DEC_SEQ, SUBLANE, -1).transpose(0, 2, 1, 3).reshape(DEC_BATCH, DEC_SEQ, -1)


def _sample_from_bsw(a):
    return a.reshape(SAMPLE_GROUPS, SUBLANE, DEC_SEQ, -1).transpose(0, 2, 1, 3).reshape(T_SAMPLE, -1)


def _pad_cols(w, n):
    return jnp.pad(w, ((0, 0), (0, n - w.shape[1])))


def _rope_tables():
    half = ROPE_DIM // 2
    pos = jnp.concatenate([jnp.repeat(jnp.arange(SEQ), BATCH),
                           jnp.tile(jnp.repeat(PAST_LEN + jnp.arange(DEC_SEQ), SUBLANE), SAMPLE_GROUPS)])
    inv = ROPE_THETA ** (-jnp.arange(half, dtype=F32) / half)
    ang = pos.astype(F32)[:, None] * inv
    cos, sin = jnp.cos(ang), jnp.sin(ang)
    ones = jnp.ones((T_ALL, HEAD_DIM - ROPE_DIM), F32)
    zeros = jnp.zeros((T_ALL, HEAD_DIM - ROPE_DIM), F32)
    zh = jnp.zeros((T_ALL, half), F32)
    two = lambda parts: jnp.tile(jnp.concatenate(parts, -1), (1, LANE // HEAD_DIM))
    return two([cos, cos, ones]), two([zh, sin, zeros]), two([-sin, zh, zeros])


def kernel(x_prompt, x_sample, state_s5, state_ssd, state_conv, cache_cmp_kv, cache_sel_kv, state_win_kv, page_table, norm_mix_even, w_in_even, s5_a_re, s5_a_im, s5_log_dt, s5_b_re, s5_b_im, s5_c_re, s5_c_im, s5_d, s5_glu_w, s5_glu_b, ssd_conv_w, ssd_conv_b, ssd_dt_bias, ssd_a_log, ssd_d, ssd_norm, w_out_even, norm_mix_odd, w_in_odd, cmp_w1, cmp_w2, cmp_pos, w_out_odd, norm_mlp, w_up, w_down, norm_final):
    h = jnp.concatenate([_prompt_from_bsw(x_prompt), _sample_from_bsw(x_sample)], 0)
    even_widths = (S5_WIDTH, SSD_INNER, SSD_CONV_DIM, _pad_lanes(SSD_HEADS))
    odd_widths = (N_HEADS * HEAD_DIM, 6 * KV_W, _pad_lanes(3 * N_HEADS))
    rope_tabs = _rope_tables()
    rope_blocks = (tuple(range(N_HEADS * HEAD_DIM // LANE)), (0, 2, 4), ())
    outs = {k: [] for k in ("s5_p", "s5_s", "ssd_p", "ssd_s", "conv_p", "conv_s",
                            "cmp_p", "cmp_s", "sel_p", "sel_s", "win_p", "win_s")}
    for layer in range(DEPTH):
        i = layer // 2
        if layer % 2 == 0:
            w_in = _pad_cols(w_in_even[i], sum(even_widths)).astype(BF16)
            u, z, xbc, dt_raw = norm_proj(h, norm_mix_even[i], w_in, even_widths)
            tabs = s5_tables(s5_a_re[i], s5_a_im[i], s5_log_dt[i], s5_b_re[i], s5_b_im[i], s5_c_re[i], s5_c_im[i],
                             s5_d[i], s5_glu_w[i], s5_glu_b[i])
            ya_p, s5_hp = s5_scan(u[:T_PROMPT], jnp.zeros((BATCH, 2 * S5_GROUPS * S5_STATE), F32), tabs,
                                  n_seq=1, t_len=S5_PROMPT_STEPS)
            ya_s, s5_hs = s5_scan(u[T_PROMPT:], _s5_state_to_lanes(state_s5[i]), tabs,
                                  n_seq=SAMPLE_GROUPS, t_len=DEC_SEQ)
            outs["s5_p"].append(_s5_state_from_lanes(s5_hp))
            outs["s5_s"].append(_s5_state_from_lanes(s5_hs))
            ssdw = (ssd_conv_w[i], ssd_conv_b[i], ssd_dt_bias[i], ssd_a_log[i], ssd_d[i], ssd_norm[i])
            dt_raw = dt_raw[:, :SSD_HEADS]
            yb_p, ssd_h, conv_new = ssd_mixer(
                _prompt_to_bsw(z[:T_PROMPT]), _prompt_to_bsw(xbc[:T_PROMPT]), _prompt_to_bsw(dt_raw[:T_PROMPT]),
                jnp.zeros((BATCH, SSD_CONV - 1, SSD_CONV_DIM), F32),
                jnp.zeros((BATCH, SSD_HEADS, SSD_HEAD_DIM, SSD_STATE), F32), *ssdw)
            outs["ssd_p"].append(ssd_h)
            outs["conv_p"].append(conv_new)
            yb_s, ssd_h, conv_new = ssd_mixer(
                _sample_to_bsw(z[T_PROMPT:]), _sample_to_bsw(xbc[T_PROMPT:]), _sample_to_bsw(dt_raw[T_PROMPT:]),
                state_conv[i], state_ssd[i], *ssdw)
            outs["ssd_s"].append(ssd_h)
            outs["conv_s"].append(conv_new)
            yb = jnp.concatenate([_prompt_from_bsw(yb_p), _sample_from_bsw(yb_s)], 0)
            y = jnp.concatenate([jnp.concatenate([ya_p, ya_s], 0), yb], -1)
            h = out_proj(h, y, w_out_even[i].astype(BF16))
        else:
            w_in = _pad_cols(w_in_odd[i], sum(odd_widths)).astype(BF16)
            q, kv, g = norm_proj(h, norm_mix_odd[i], w_in, odd_widths, rope_tabs, rope_blocks)
            kv_p = _prompt_to_bsw(kv[:T_PROMPT])
            new_cmp, new_sel, new_win = (_kv_rows(kv_p, br) for br in range(3))
            kc = compress_blocks(new_cmp[:, :, 0], cmp_w1[i][0], cmp_w2[i][0], cmp_pos[i][0])
            vc = compress_blocks(new_cmp[:, :, 1], cmp_w1[i][1], cmp_w2[i][1], cmp_pos[i][1])
            pad_c = lambda a: jnp.pad(a.reshape(BATCH, N_CMP_PROMPT, KV_W), ((0, 0), (0, LANE - N_CMP_PROMPT), (0, 0)))
            y_p = nsa_prompt(q[:T_PROMPT].reshape(SEQ, -1), g[:T_PROMPT].reshape(SEQ, -1),
                             kv[:T_PROMPT].reshape(SEQ, -1), pad_c(kc), pad_c(vc)).reshape(T_PROMPT, -1)
            outs["cmp_p"].append(new_cmp)
            outs["sel_p"].append(new_sel)
            outs["win_p"].append(new_win[:, SEQ - WINDOW:])
            past_cmp = cache_cmp_kv[i][page_table].reshape(DEC_BATCH, PAST_LEN, 2, N_KV, HEAD_DIM)
            past_sel = cache_sel_kv[i][page_table].reshape(DEC_BATCH, PAST_LEN, 2, N_KV, HEAD_DIM)
            y_s, a, b, c = nsa_sample_mix(_sample_to_bsw(q[T_PROMPT:]), _sample_to_bsw(kv[T_PROMPT:]),
                                          _sample_to_bsw(g[T_PROMPT:]), past_cmp, past_sel, state_win_kv[i],
                                          cmp_w1[i], cmp_w2[i], cmp_pos[i])
            outs["cmp_s"].append(a)
            outs["sel_s"].append(b)
            outs["win_s"].append(c)
            y = jnp.concatenate([y_p, _sample_from_bsw(y_s)], 0)
            h = out_proj(h, y, w_out_odd[i].astype(BF16))
        h = mlp(h, norm_mlp[layer], w_up[layer].astype(BF16), w_down[layer].astype(BF16),
                norm_final, final=(layer == DEPTH - 1))
    y_prompt, y_sample = _prompt_to_bsw(h[:T_PROMPT]), _sample_to_bsw(h[T_PROMPT:])
    st = {k: jnp.stack(v) for k, v in outs.items()}
    return (y_prompt, y_sample, st["s5_p"], st["s5_s"], st["ssd_p"], st["ssd_s"], st["conv_p"], st["conv_s"],
            st["cmp_p"], st["cmp_s"], st["sel_p"], st["sel_s"], st["win_p"], st["win_s"])
```

```python
import functools

import jax
import jax.numpy as jnp
import numpy as np
from jax import lax
from jax.experimental import pallas as pl
from jax.experimental.pallas import tpu as pltpu

D_MODEL = 1024
BATCH = 8
SEQ = 2048
DEPTH = 4
DEC_BATCH = 128
DEC_SEQ = 4
PAST_LEN = 2048
PAGE_SIZE = 128

N_SSM = (DEPTH + 1) // 2
N_ATT = DEPTH // 2
NORM_EPS = 1e-5
D_FF = 4 * D_MODEL
NEG_INF = -1e30

S5_WIDTH = D_MODEL // 2
S5_GROUP = 16
S5_GROUPS = S5_WIDTH // S5_GROUP
S5_STATE = 64

SSD_INNER = D_MODEL
SSD_HEAD_DIM = 64
SSD_HEADS = SSD_INNER // SSD_HEAD_DIM
SSD_STATE = 128
SSD_GROUPS = 4
SSD_CONV = 4
SSD_CONV_DIM = SSD_INNER + 2 * SSD_GROUPS * SSD_STATE
SSD_CHUNK = 128
MIX_EVEN = S5_WIDTH + SSD_INNER
IN_EVEN = S5_WIDTH + SSD_INNER + SSD_CONV_DIM + SSD_HEADS

N_HEADS = 16
HEAD_DIM = D_MODEL // N_HEADS
N_KV = 2
HEADS_PER_KV = N_HEADS // N_KV
KV_W = N_KV * HEAD_DIM
ROPE_DIM = HEAD_DIM // 4
ROPE_THETA = 500000.0
ATT_SCALE = HEAD_DIM ** -0.5
CMP_LEN = 32
CMP_STRIDE = 16
CMP_HIDDEN = 2 * HEAD_DIM
SEL_LEN = 64
SEL_TOPK = 16
FORCE_BONUS = 1e4
WINDOW = 512
ATT_QBLOCK = 128
SEL_QBLOCK = 64
IN_ODD = N_HEADS * HEAD_DIM + 6 * KV_W + 3 * N_HEADS

T_PROMPT = BATCH * SEQ
T_SAMPLE = DEC_BATCH * DEC_SEQ
T_ALL = T_PROMPT + T_SAMPLE

LANE = 128
SUBLANE = 8
TOKEN_TILE = 512
FF_CHUNK = 1024
VMEM_LIMIT = 56 * 1024 * 1024

S5_SLAB_GROUPS = LANE // S5_GROUP
S5_SLABS = S5_GROUPS // S5_SLAB_GROUPS
S5_SLAB_STATE = S5_SLAB_GROUPS * S5_STATE
S5_PROMPT_STEPS = 256
SAMPLE_GROUPS = DEC_BATCH // SUBLANE

NSA_Q_TILE = 256
NSA_K_TILE = 256
N_SLC_PROMPT = SEQ // SEL_LEN
N_CMP_PROMPT = SEQ // CMP_STRIDE - CMP_LEN // CMP_STRIDE + 1

F32 = jnp.float32
BF16 = jnp.bfloat16


def _pad_lanes(n):
    return -(-n // LANE) * LANE


def _rms(x, g):
    return x * lax.rsqrt(jnp.mean(x * x, -1, keepdims=True) + NORM_EPS) * g


def _resident(shape):
    return pl.BlockSpec(shape, lambda *_: (0,) * len(shape), pipeline_mode=pl.Buffered(1))


def _rows(width):
    return pl.BlockSpec((TOKEN_TILE, width), lambda i: (i, 0))


def _dot(a, b):
    return jnp.dot(a, b, preferred_element_type=F32)


def _dot_nt(a, b):
    return lax.dot_general(a, b, (((1,), (1,)), ((), ())), preferred_element_type=F32)


def _split_bf16(x):
    hi = x.astype(BF16)
    return hi, (x - hi.astype(F32)).astype(BF16)


_PARAMS = pltpu.CompilerParams(dimension_semantics=("arbitrary",), vmem_limit_bytes=VMEM_LIMIT)
_PARAMS2 = pltpu.CompilerParams(dimension_semantics=("arbitrary", "arbitrary"), vmem_limit_bytes=VMEM_LIMIT)


def _rope_block(x, cos_f, sin_a, sin_b):
    return x * cos_f + pltpu.roll(x, ROPE_DIM // 2, 1) * sin_a + pltpu.roll(x, LANE - ROPE_DIM // 2, 1) * sin_b


def _norm_proj_body(x_ref, g_ref, w_ref, *refs, widths, rope_blocks):
    if rope_blocks is None:
        o_refs = refs
    else:
        cos_ref, sa_ref, sb_ref = refs[:3]
        o_refs = refs[3:]
    xn = _rms(x_ref[...], g_ref[...]).astype(BF16)
    off = 0
    for idx, (o_ref, wd) in enumerate(zip(o_refs, widths)):
        o_ref[...] = _dot(xn, w_ref[:, off:off + wd])
        off += wd
        if rope_blocks is not None:
            for blk in rope_blocks[idx]:
                cols = slice(blk * LANE, (blk + 1) * LANE)
                o_ref[:, cols] = _rope_block(o_ref[:, cols], cos_ref[...], sa_ref[...], sb_ref[...])


def norm_proj(x, g, w, widths, rope_tabs=None, rope_blocks=None):
    n = sum(widths)
    in_specs = [_rows(D_MODEL), _resident((1, D_MODEL)), _resident((D_MODEL, n))]
    args = [x, g.reshape(1, D_MODEL), w]
    if rope_blocks is not None:
        in_specs += [_rows(LANE)] * 3
        args += list(rope_tabs)
    return pl.pallas_call(
        functools.partial(_norm_proj_body, widths=widths, rope_blocks=rope_blocks),
        grid=(T_ALL // TOKEN_TILE,),
        in_specs=in_specs,
        out_specs=[_rows(wd) for wd in widths],
        out_shape=[jax.ShapeDtypeStruct((T_ALL, wd), F32) for wd in widths],
        compiler_params=_PARAMS,
        name="norm_proj",
    )(*args)


def _out_proj_body(x_ref, *refs, widths):
    y_refs, w_ref, o_ref = refs[:len(widths)], refs[len(widths)], refs[len(widths) + 1]
    acc = x_ref[...]
    off = 0
    for y_ref, wd in zip(y_refs, widths):
        acc = acc + _dot(y_ref[...].astype(BF16), w_ref[off:off + wd, :])
        off += wd
    o_ref[...] = acc


def out_proj(x, ys, w):
    widths = tuple(y.shape[1] for y in ys)
    return pl.pallas_call(
        functools.partial(_out_proj_body, widths=widths),
        grid=(T_ALL // TOKEN_TILE,),
        in_specs=[_rows(D_MODEL)] + [_rows(wd) for wd in widths] + [_resident((sum(widths), D_MODEL))],
        out_specs=_rows(D_MODEL),
        out_shape=jax.ShapeDtypeStruct((T_ALL, D_MODEL), F32),
        compiler_params=_PARAMS,
        name="out_proj",
    )(x, *ys, w)


def _mlp_body(x_ref, g_ref, wu_ref, wd_ref, gf_ref, o_ref, *, final):
    x = x_ref[...]
    xn = _rms(x, g_ref[...]).astype(BF16)
    acc = x
    for c in range(D_FF // FF_CHUNK):
        cols = slice(c * FF_CHUNK, (c + 1) * FF_CHUNK)
        h = jnp.maximum(_dot(xn, wu_ref[:, cols]), 0.0)
        acc = acc + _dot((h * h).astype(BF16), wd_ref[cols, :])
    if final:
        acc = _rms(acc, gf_ref[...])
    o_ref[...] = acc


def mlp(x, g, w_up, w_down, g_final, final):
    return pl.pallas_call(
        functools.partial(_mlp_body, final=final),
        grid=(T_ALL // TOKEN_TILE,),
        in_specs=[_rows(D_MODEL), _resident((1, D_MODEL)), _resident((D_MODEL, D_FF)),
                  _resident((D_FF, D_MODEL)), _resident((1, D_MODEL))],
        out_specs=_rows(D_MODEL),
        out_shape=jax.ShapeDtypeStruct((T_ALL, D_MODEL), F32),
        compiler_params=_PARAMS,
        name="mlp",
    )(x, g.reshape(1, D_MODEL), w_up, w_down, g_final.reshape(1, D_MODEL))


def _cmul(ar, ai, br, bi):
    return ar * br - ai * bi, ar * bi + ai * br


def s5_tables(a_re, a_im, log_dt, b_re, b_im, c_re, c_im, d_skip, glu_w, glu_b):
    dt = jnp.exp(log_dt)[:, None]
    mag = jnp.exp(a_re * dt)
    abar_re, abar_im = mag * jnp.cos(a_im * dt), mag * jnp.sin(a_im * dt)
    den = a_re * a_re + a_im * a_im
    f_re = ((abar_re - 1.0) * a_re + abar_im * a_im) / den
    f_im = (abar_im * a_re - (abar_re - 1.0) * a_im) / den
    bbar_re, bbar_im = _cmul(f_re[..., None], f_im[..., None], b_re, b_im)
    sg, ns = S5_SLAB_GROUPS, S5_SLABS
    eye = jnp.eye(sg, dtype=F32)
    a_tab = jnp.stack([abar_re.reshape(ns, S5_SLAB_STATE), abar_im.reshape(ns, S5_SLAB_STATE)], 1)

    def in_blockdiag(bb):
        return jnp.einsum('jgpc,gh->jgchp', bb.reshape(ns, sg, S5_STATE, S5_GROUP), eye).reshape(ns, LANE, S5_SLAB_STATE)

    def out_blockdiag(cc):
        return jnp.einsum('jgcp,gh->jhpgc', cc.reshape(ns, sg, S5_GROUP, S5_STATE), eye).reshape(ns, S5_SLAB_STATE, LANE)

    b_bd = jnp.concatenate([in_blockdiag(bbar_re), in_blockdiag(bbar_im)], -1).astype(BF16)
    c_bd = jnp.concatenate([out_blockdiag(c_re), out_blockdiag(-c_im)], 1).astype(BF16)
    g_bd = jnp.einsum('jgcke,gh->jgckhe', glu_w.reshape(ns, sg, S5_GROUP, 2, S5_GROUP), eye)
    g_bd = g_bd.reshape(ns, LANE, 2 * LANE).astype(BF16)
    g_b = glu_b.reshape(ns, sg, 2, S5_GROUP).transpose(0, 2, 1, 3).reshape(ns, 1, 2 * LANE)
    return a_tab, b_bd, c_bd, d_skip.reshape(ns, 1, LANE), g_bd, g_b


def _s5_body(u_ref, h0_ref, a_ref, b_ref, c_ref, d_ref, gw_ref, gb_ref, o_ref, hl_ref, st_ref, h_ref, *,
             n_seq, t_len):
    w = S5_SLAB_STATE

    @pl.when(pl.program_id(1) == 0)
    def _():
        h_ref[...] = h0_ref[...]

    ub = u_ref[...]
    st_ref[...] = _dot(ub.astype(BF16), b_ref[0])
    ar = jnp.broadcast_to(a_ref[0, 0:1, :], (SUBLANE, w))
    ai = jnp.broadcast_to(a_ref[0, 1:2, :], (SUBLANE, w))
    for s in range(n_seq):
        base = s * t_len * SUBLANE
        rows = slice(s * SUBLANE, (s + 1) * SUBLANE)

        def step(t, carry, base=base):
            hr, hi = carry
            r = pl.ds(pl.multiple_of(base + t * SUBLANE, SUBLANE), SUBLANE)
            nhr = ar * hr - ai * hi + st_ref[r, 0:w]
            nhi = ar * hi + ai * hr + st_ref[r, w:2 * w]
            st_ref[r, 0:w] = nhr
            st_ref[r, w:2 * w] = nhi
            return nhr, nhi

        hr, hi = lax.fori_loop(0, t_len, step, (h_ref[rows, 0:w], h_ref[rows, w:2 * w]), unroll=min(t_len, 8))
        h_ref[rows, 0:w] = hr
        h_ref[rows, w:2 * w] = hi
    y = _dot(st_ref[...].astype(BF16), c_ref[0]) + d_ref[0] * ub
    zg = _dot(y.astype(BF16), gw_ref[0]) + gb_ref[0]
    o_ref[...] = zg[:, :LANE] * jax.nn.sigmoid(zg[:, LANE:])
    hl_ref[...] = h_ref[...]


def s5_scan(u, h0, tabs, *, n_seq, t_len):
    blk = n_seq * t_len * SUBLANE
    n_chunks = u.shape[0] // blk
    assert n_chunks * blk == u.shape[0] and (n_seq == 1 or n_chunks == 1)
    hb = n_seq * SUBLANE
    slab = lambda shape: pl.BlockSpec((1,) + shape, lambda j, c: (j, 0, 0))
    return pl.pallas_call(
        functools.partial(_s5_body, n_seq=n_seq, t_len=t_len),
        grid=(S5_SLABS, n_chunks),
        in_specs=[pl.BlockSpec((blk, LANE), lambda j, c: (c, j)),
                  pl.BlockSpec((hb, 2 * S5_SLAB_STATE), lambda j, c: (0, j)),
                  slab((2, S5_SLAB_STATE)), slab((LANE, 2 * S5_SLAB_STATE)), slab((2 * S5_SLAB_STATE, LANE)),
                  slab((1, LANE)), slab((LANE, 2 * LANE)), slab((1, 2 * LANE))],
        out_specs=[pl.BlockSpec((blk, LANE), lambda j, c: (c, j)),
                   pl.BlockSpec((hb, 2 * S5_SLAB_STATE), lambda j, c: (0, j))],
        out_shape=[jax.ShapeDtypeStruct(u.shape, F32), jax.ShapeDtypeStruct(h0.shape, F32)],
        scratch_shapes=[pltpu.VMEM((blk, 2 * S5_SLAB_STATE), F32), pltpu.VMEM((hb, 2 * S5_SLAB_STATE), F32)],
        compiler_params=_PARAMS2,
        name="s5_scan",
    )(u, h0, *tabs)


def _s5_state_to_lanes(h):
    bt = h.shape[0]
    return h.reshape(bt, S5_SLABS, S5_SLAB_STATE, 2).transpose(0, 1, 3, 2).reshape(bt, -1)


def _s5_state_from_lanes(h):
    bt = h.shape[0]
    return h.reshape(bt, S5_SLABS, 2, S5_SLAB_STATE).transpose(0, 1, 3, 2).reshape(bt, S5_GROUPS, S5_STATE, 2)


SSD_BC = SSD_GROUPS * SSD_STATE
SSD_GROUP_W = SSD_INNER // SSD_GROUPS
CONV_PAD = SUBLANE


def _ssd_consts():
    tri = np.tril(np.ones((SSD_CHUNK, SSD_CHUNK), np.float32))
    expand = np.zeros((LANE, SSD_INNER), np.float32)
    for h in range(SSD_HEADS):
        expand[h, h * SSD_HEAD_DIM:(h + 1) * SSD_HEAD_DIM] = 1.0
    return jnp.asarray(tri, BF16), jnp.asarray(expand, BF16)


def _split3_dot(m, x):
    hi, r = _split_bf16(x)
    x2 = x - hi.astype(F32) - r.astype(F32)
    return _dot(m, hi) + _dot(m, r) + _dot(m, x2.astype(BF16))


def _expand_heads(x, eh):
    hi, lo = _split_bf16(x)
    return _dot(hi, eh) + _dot(lo, eh)


def _ssd_prompt_body(z_ref, x_ref, dt_ref, cw_ref, cb_ref, dtb_ref, alog_ref, d_ref, ng_ref, tri_ref, eh_ref,
                     y_ref, hl_ref, cl_ref, xs_ref, h_ref):
    n = SSD_CHUNK
    k0 = CONV_PAD - (SSD_CONV - 1)

    @pl.when(pl.program_id(1) == 0)
    def _():
        h_ref[...] = jnp.zeros(h_ref.shape, F32)
        xs_ref[0:CONV_PAD, :] = jnp.zeros((CONV_PAD, SSD_CONV_DIM), F32)

    xs_ref[CONV_PAD:CONV_PAD + n, :] = x_ref[...]
    conv = cb_ref[...]
    for k in range(SSD_CONV):
        conv = conv + xs_ref[k0 + k:k0 + k + n, :] * cw_ref[k:k + 1, :]
    tail = xs_ref[k0 + n:CONV_PAD + n, :]
    xs_ref[k0:CONV_PAD, :] = tail
    cl_ref[0] = tail
    xa = conv * jax.nn.sigmoid(conv)
    x = xa[:, :SSD_INNER]

    dt = jax.nn.softplus(dt_ref[...] + dtb_ref[...])
    a_dt = dt * (-jnp.exp(alog_ref[...]))
    a_cs = _split3_dot(tri_ref[...], a_dt)
    a_cs_t = a_cs.T
    dt_t = dt.T
    a_tot = a_cs[n - 1:n, :]
    eh = eh_ref[...]
    decay_in = _expand_heads(jnp.exp(a_cs), eh)
    xw = x * _expand_heads(jnp.exp(a_tot - a_cs) * dt, eh)
    lower = (lax.broadcasted_iota(jnp.int32, (n, n), 0) >= lax.broadcasted_iota(jnp.int32, (n, n), 1))
    low = lax.broadcasted_iota(jnp.int32, (n, LANE), 1) < SSD_HEAD_DIM
    heads_per_group = SSD_HEADS // SSD_GROUPS
    for g in range(SSD_GROUPS):
        b_g = xa[:, SSD_INNER + g * SSD_STATE:SSD_INNER + (g + 1) * SSD_STATE].astype(BF16)
        c_g = xa[:, SSD_INNER + SSD_BC + g * SSD_STATE:SSD_INNER + SSD_BC + (g + 1) * SSD_STATE].astype(BF16)
        gcols = slice(g * SSD_GROUP_W, (g + 1) * SSD_GROUP_W)
        h_g = h_ref[gcols, :]
        cb = _dot_nt(c_g, b_g)
        y_off = _dot_nt(c_g, h_g.astype(BF16)) * decay_in[:, gcols]
        for pr in range(heads_per_group // 2):
            cols = slice(g * SSD_GROUP_W + pr * LANE, g * SSD_GROUP_W + (pr + 1) * LANE)
            x_pair = x[:, cols]
            y_pair = y_off[:, pr * LANE:(pr + 1) * LANE] + x_pair * d_ref[:, cols]
            for side in range(2):
                h = g * heads_per_group + 2 * pr + side
                seg = jnp.where(lower, jnp.exp(a_cs[:, h:h + 1] - a_cs_t[h:h + 1, :]), 0.0)
                m = (cb * seg * dt_t[h:h + 1, :]).astype(BF16)
                x_side = jnp.where(low if side == 0 else ~low, x_pair, 0.0).astype(BF16)
                y_pair = y_pair + _dot(m, x_side)
            y_ref[:, cols] = y_pair
        upd = _dot(xw[:, gcols].T.astype(BF16), b_g)
        for hh in range(heads_per_group):
            h = g * heads_per_group + hh
            rows = slice(g * SSD_GROUP_W + hh * SSD_HEAD_DIM, g * SSD_GROUP_W + (hh + 1) * SSD_HEAD_DIM)
            scale = jnp.exp(jnp.broadcast_to(a_tot[:, h:h + 1], (SSD_HEAD_DIM, SSD_STATE)))
            h_ref[rows, :] = h_ref[rows, :] * scale + upd[hh * SSD_HEAD_DIM:(hh + 1) * SSD_HEAD_DIM, :]
    z = z_ref[...]
    y_ref[...] = _rms(y_ref[...] * (z * jax.nn.sigmoid(z)), ng_ref[...])
    hl_ref[0] = h_ref[...]


def ssd_prompt(z, xbc, dt_raw, conv_w, conv_b, dt_bias, a_log, d_skip, norm_g):
    n = SSD_CHUNK
    nb = z.shape[1] // SSD_INNER
    tri, eh = _ssd_consts()
    lanes = lambda v: jnp.pad(v, (0, LANE - v.shape[0])).reshape(1, LANE)
    chunk = lambda w: pl.BlockSpec((n, w), lambda b, c: (c, b))
    return pl.pallas_call(
        _ssd_prompt_body,
        grid=(nb, SEQ // n),
        in_specs=[chunk(SSD_INNER), chunk(SSD_CONV_DIM), chunk(LANE),
                  _resident((SSD_CONV, SSD_CONV_DIM)), _resident((1, SSD_CONV_DIM)), _resident((1, LANE)),
                  _resident((1, LANE)), _resident((1, SSD_INNER)), _resident((1, SSD_INNER)),
                  _resident(tri.shape), _resident(eh.shape)],
        out_specs=[chunk(SSD_INNER),
                   pl.BlockSpec((1, SSD_INNER, SSD_STATE), lambda b, c: (b, 0, 0)),
                   pl.BlockSpec((1, SSD_CONV - 1, SSD_CONV_DIM), lambda b, c: (b, 0, 0))],
        out_shape=[jax.ShapeDtypeStruct((SEQ, nb * SSD_INNER), F32),
                   jax.ShapeDtypeStruct((nb, SSD_INNER, SSD_STATE), F32),
                   jax.ShapeDtypeStruct((nb, SSD_CONV - 1, SSD_CONV_DIM), F32)],
        scratch_shapes=[pltpu.VMEM((CONV_PAD + n, SSD_CONV_DIM), F32), pltpu.VMEM((SSD_INNER, SSD_STATE), F32)],
        compiler_params=_PARAMS2,
        name="ssd_prompt",
    )(z, xbc, dt_raw, conv_w, conv_b.reshape(1, -1), lanes(dt_bias), lanes(a_log),
      jnp.repeat(d_skip, SSD_HEAD_DIM).reshape(1, -1), norm_g.reshape(1, -1), tri, eh)


def _nsa_consts():
    nq = N_SLC_PROMPT
    ci = np.arange(LANE)[None, :]
    sj = np.arange(nq)[:, None]
    overlap_t = ((ci * CMP_STRIDE < (sj + 1) * SEL_LEN) & (ci * CMP_STRIDE + CMP_LEN > sj * SEL_LEN)
                 & (ci < N_CMP_PROMPT)).astype(np.float32)
    keys = np.arange(SEQ)
    expand = (keys[:, None] // SEL_LEN == np.arange(nq)[None, :]).astype(np.float32)
    expand = expand.reshape(SEQ // NSA_K_TILE, NSA_K_TILE, nq)
    gate = np.zeros((3, N_HEADS * HEAD_DIM, LANE), np.float32)
    for br in range(3):
        for h in range(N_HEADS):
            gate[br, h * HEAD_DIM:(h + 1) * HEAD_DIM, h * 3 + br] = 1.0
    return jnp.asarray(overlap_t, BF16), jnp.asarray(expand, BF16), jnp.asarray(gate, BF16)


def _k_variants(k):
    low = lax.broadcasted_iota(jnp.int32, k.shape, 1) < HEAD_DIM
    k = k * ATT_SCALE
    k_sw = pltpu.roll(k, HEAD_DIM, 1)
    kl = (jnp.where(low, k, 0.0).astype(BF16), jnp.where(low, k_sw, 0.0).astype(BF16))
    kr = (jnp.where(low, 0.0, k_sw).astype(BF16), jnp.where(low, 0.0, k).astype(BF16))
    return kl, kr


def _vt_variants(v):
    vt = v.T
    zero = jnp.zeros((HEAD_DIM, v.shape[0]), F32)
    top, bot = vt[:HEAD_DIM], vt[HEAD_DIM:]
    vl = (jnp.concatenate([top, zero], 0).astype(BF16), jnp.concatenate([bot, zero], 0).astype(BF16))
    vr = (jnp.concatenate([zero, top], 0).astype(BF16), jnp.concatenate([zero, bot], 0).astype(BF16))
    return vl, vr


def _nsa_prompt_body(q_ref, g_ref, kv_ref, kc_ref, vc_ref, ov_ref, ex_ref, eg_ref, o_ref,
                     kl_ref, kr_ref, vl_ref, vr_ref, acc_ref, m_ref, l_ref, ob_ref):
    tq = NSA_Q_TILE
    tk = NSA_K_TILE
    n_pairs = HEADS_PER_KV // 2
    qi = pl.program_id(1)
    q0 = qi * tq

    @pl.when(qi == 0)
    def _():
        for br in range(2):
            c0 = (2 + 2 * br) * LANE
            kl, kr = _k_variants(kv_ref[:, c0:c0 + LANE])
            for g in range(N_KV):
                kl_ref[br, g] = kl[g]
                kr_ref[br, g] = kr[g]
            for t in range(SEQ // tk):
                vl, vr = _vt_variants(kv_ref[t * tk:(t + 1) * tk, c0 + LANE:c0 + 2 * LANE])
                for g in range(N_KV):
                    vl_ref[br, g, t] = vl[g]
                    vr_ref[br, g, t] = vr[g]

    row_low = lax.broadcasted_iota(jnp.int32, (LANE, tq), 0) < HEAD_DIM
    kpos_t = lax.broadcasted_iota(jnp.int32, (tk, tq), 0)
    qpos_t = q0 + lax.broadcasted_iota(jnp.int32, (tk, tq), 1)
    kcl, kcr = _k_variants(kc_ref[0])
    vcl, vcr = _vt_variants(vc_ref[0])
    n_iota = lax.broadcasted_iota(jnp.int32, (LANE, tq), 0)
    cmp_mask = ((n_iota * CMP_STRIDE + (CMP_LEN - 1) <= q0 + lax.broadcasted_iota(jnp.int32, (LANE, tq), 1))
                & (n_iota < N_CMP_PROMPT))

    def pair_q(g, pr):
        c0 = (g * n_pairs + pr) * LANE
        return q_ref[:, c0:c0 + LANE].astype(BF16)

    def flash(br, g, n_tiles, tile0, mask_fn, reverse):
        m_ref[...] = jnp.full(m_ref.shape, NEG_INF, F32)
        l_ref[...] = jnp.zeros(l_ref.shape, F32)
        acc_ref[...] = jnp.zeros(acc_ref.shape, F32)

        def tile(kt, carry):
            if reverse:
                kt = n_tiles - 1 - kt
            kt = tile0 + kt
            k0 = pl.multiple_of(kt * tk, tk)
            mask = mask_fn(kt, k0)
            for pr in range(n_pairs):
                qp = pair_q(g, pr)
                ps = []
                alpha = []
                for side, k_ref in enumerate((kl_ref, kr_ref)):
                    hh = 2 * pr + side
                    s = jnp.where(mask, _dot_nt(k_ref[br, g, pl.ds(k0, tk), :], qp), NEG_INF)
                    m_old = m_ref[hh:hh + 1, :]
                    m_new = jnp.maximum(m_old, jnp.max(s, 0, keepdims=True))
                    p = jnp.exp(s - m_new)
                    a = jnp.exp(m_old - m_new)
                    l_ref[hh:hh + 1, :] = a * l_ref[hh:hh + 1, :] + jnp.sum(p, 0, keepdims=True)
                    m_ref[hh:hh + 1, :] = m_new
                    ps.append(p.astype(BF16))
                    alpha.append(a)
                pv = _dot(vl_ref[br, g, kt], ps[0]) + _dot(vr_ref[br, g, kt], ps[1])
                acc_ref[pr] = acc_ref[pr] * jnp.where(row_low, alpha[0], alpha[1]) + pv
            return carry

        lax.fori_loop(0, n_tiles, tile, 0)
        for pr in range(n_pairs):
            r0 = (g * n_pairs + pr) * LANE
            l_pair = jnp.where(row_low, l_ref[2 * pr:2 * pr + 1, :], l_ref[2 * pr + 1:2 * pr + 2, :])
            ob_ref[br + 1, r0:r0 + LANE, :] = acc_ref[pr] / l_pair

    for g in range(N_KV):
        psum = jnp.zeros((LANE, tq), F32)
        for pr in range(n_pairs):
            qp = pair_q(g, pr)
            ps = []
            for kc in (kcl[g], kcr[g]):
                s = jnp.where(cmp_mask, _dot_nt(kc, qp), NEG_INF)
                p = jnp.where(cmp_mask, jnp.exp(s - jnp.max(s, 0, keepdims=True)), 0.0)
                p = p / jnp.maximum(jnp.sum(p, 0, keepdims=True), 1e-30)
                psum = psum + p
                ps.append(p.astype(BF16))
            r0 = (g * n_pairs + pr) * LANE
            ob_ref[0, r0:r0 + LANE, :] = _dot(vcl[g], ps[0]) + _dot(vcr[g], ps[1])

        p_hi, p_lo = _split_bf16(psum)
        imp = _dot(ov_ref[...], p_hi) + _dot(ov_ref[...], p_lo)
        shp = (N_SLC_PROMPT, tq)
        jj = lax.broadcasted_iota(jnp.int32, shp, 0)
        qp_t = q0 + lax.broadcasted_iota(jnp.int32, shp, 1)
        qb_t = qp_t // SEL_LEN
        forced = (jj == 0) | (jj == qb_t) | (jj == qb_t - 1)
        score = jnp.where(jj * SEL_LEN <= qp_t, imp + jnp.where(forced, FORCE_BONUS, 0.0), NEG_INF)
        rank = jnp.zeros(shp, F32)
        for i in range(N_SLC_PROMPT):
            row = score[i:i + 1, :]
            ahead = (row > score) | ((row == score) & (jj > i))
            rank = rank + jnp.where(ahead, 1.0, 0.0)
        sel_t = jnp.where(rank < SEL_TOPK, 1.0, 0.0).astype(BF16)

        def sel_mask(kt, k0, sel_t=sel_t):
            chosen = _dot(ex_ref[kt], sel_t) > 0.5
            return chosen & (k0 + kpos_t <= qpos_t)

        flash(0, g, (q0 + tq) // tk, 0, sel_mask, reverse=False)

        def win_mask(kt, k0):
            rel = qpos_t - (k0 + kpos_t)
            return (rel >= 0) & (rel < WINDOW)

        w0 = jnp.maximum(q0 - WINDOW, 0) // tk
        flash(1, g, (q0 + tq) // tk - w0, w0, win_mask, reverse=True)

    g_hi, g_lo = _split_bf16(jax.nn.sigmoid(g_ref[...]))
    out_t = jnp.zeros((N_HEADS * HEAD_DIM, tq), F32)
    for br in range(3):
        out_t = out_t + (_dot_nt(eg_ref[br], g_hi) + _dot_nt(eg_ref[br], g_lo)) * ob_ref[br]
    for c in range(N_HEADS * HEAD_DIM // LANE):
        o_ref[:, c * LANE:(c + 1) * LANE] = out_t[c * LANE:(c + 1) * LANE, :].T


def nsa_prompt(q, gate, kv, kc, vc):
    tq = NSA_Q_TILE
    qw = N_HEADS * HEAD_DIM
    ov, ex, eg = _nsa_consts()
    return pl.pallas_call(
        _nsa_prompt_body,
        grid=(kc.shape[0], SEQ // tq),
        in_specs=[pl.BlockSpec((tq, qw), lambda b, i: (i, b)),
                  pl.BlockSpec((tq, LANE), lambda b, i: (i, b)),
                  pl.BlockSpec((SEQ, 6 * KV_W), lambda b, i: (0, b)),
                  pl.BlockSpec((1, LANE, LANE), lambda b, i: (b, 0, 0)),
                  pl.BlockSpec((1, LANE, LANE), lambda b, i: (b, 0, 0)),
                  _resident(ov.shape), _resident(ex.shape), _resident(eg.shape)],
        out_specs=pl.BlockSpec((tq, qw), lambda b, i: (i, b)),
        out_shape=jax.ShapeDtypeStruct((SEQ, kc.shape[0] * qw), F32),
        scratch_shapes=[pltpu.VMEM((2, N_KV, SEQ, LANE), BF16), pltpu.VMEM((2, N_KV, SEQ, LANE), BF16),
                        pltpu.VMEM((2, N_KV, SEQ // NSA_K_TILE, LANE, NSA_K_TILE), BF16),
                        pltpu.VMEM((2, N_KV, SEQ // NSA_K_TILE, LANE, NSA_K_TILE), BF16),
                        pltpu.VMEM((HEADS_PER_KV // 2, LANE, tq), F32),
                        pltpu.VMEM((HEADS_PER_KV, tq), F32), pltpu.VMEM((HEADS_PER_KV, tq), F32),
                        pltpu.VMEM((3, qw, tq), F32)],
        compiler_params=_PARAMS2,
        name="nsa_prompt",
    )(q, gate, kv, kc, vc, ov, ex, eg)


def masked_softmax(s, mask):
    p = jax.nn.softmax(jnp.where(mask, s, NEG_INF), axis=-1)
    return jnp.where(mask, p, 0.0)


def _segsum_exp(a):
    t = a.shape[-1]
    cs = jnp.cumsum(a, -1)
    tril = np.tril(np.ones((t, t), dtype=bool))
    return jnp.exp(jnp.where(tril, cs[..., :, None] - cs[..., None, :], -jnp.inf))


def _block(n, pref):
    return pref if n % pref == 0 else n


def ssd_mixer(z, xbc, dt_raw, conv_buf, h0, conv_w, conv_b, dt_bias, a_log, d_skip, norm_g):
    bt, s, _ = xbc.shape
    xpad = jnp.concatenate([conv_buf, xbc], 1)
    conv = conv_b + sum(xpad[:, k:k + s] * conv_w[k] for k in range(SSD_CONV))
    new_buf = xpad[:, s:]
    xbc_a = jax.nn.silu(conv)
    n_bc = SSD_GROUPS * SSD_STATE
    x = xbc_a[..., :SSD_INNER].reshape(bt, s, SSD_HEADS, SSD_HEAD_DIM)
    bm = xbc_a[..., SSD_INNER:SSD_INNER + n_bc]
    cm = xbc_a[..., SSD_INNER + n_bc:]
    dt = jax.nn.softplus(dt_raw + dt_bias)
    a = -jnp.exp(a_log)
    q = _block(s, SSD_CHUNK)
    nc = s // q
    r = SSD_HEADS // SSD_GROUPS
    xdt = (x * dt[..., None]).reshape(bt, nc, q, SSD_GROUPS, r, SSD_HEAD_DIM)
    bm = bm.reshape(bt, nc, q, SSD_GROUPS, SSD_STATE)
    cm = cm.reshape(bt, nc, q, SSD_GROUPS, SSD_STATE)
    a_dt = (dt * a).reshape(bt, nc, q, SSD_GROUPS, r).transpose(0, 3, 4, 1, 2)
    a_cs = jnp.cumsum(a_dt, -1)
    lmat = _segsum_exp(a_dt)
    cb = jnp.einsum('bclgn,bcsgn->bcgls', cm, bm)
    y_diag = jnp.einsum('bcgls,bgrcls,bcsgrp->bclgrp', cb, lmat, xdt)
    decay = jnp.exp(a_cs[..., -1:] - a_cs)
    states = jnp.einsum('bclgn,bgrcl,bclgrp->bcgrpn', bm, decay, xdt)
    h0g = h0.reshape(bt, 1, SSD_GROUPS, r, SSD_HEAD_DIM, SSD_STATE)
    states = jnp.concatenate([h0g, states], 1)
    chunk_decay = _segsum_exp(jnp.pad(a_cs[..., -1], [(0, 0)] * 3 + [(1, 0)]))
    states = jnp.einsum('bgrzc,bcgrpn->bzgrpn', chunk_decay, states)
    y_off = jnp.einsum('bclgn,bcgrpn,bgrcl->bclgrp', cm, states[:, :-1], jnp.exp(a_cs))
    y = (y_diag + y_off).reshape(bt, s, SSD_HEADS, SSD_HEAD_DIM) + x * d_skip[:, None]
    y = y.reshape(bt, s, SSD_INNER) * jax.nn.silu(z)
    y = _rms(y, norm_g)
    h_last = states[:, -1].reshape(bt, SSD_HEADS, SSD_HEAD_DIM, SSD_STATE)
    return y, h_last, new_buf


def compress_blocks(k, w1, w2, pe):
    bt, t = k.shape[:2]
    ratio = CMP_LEN // CMP_STRIDE
    n_chunk = t // CMP_STRIDE
    n_cmp = n_chunk - ratio + 1
    ch = k[:, :n_chunk * CMP_STRIDE].reshape(bt, n_chunk, CMP_STRIDE, N_KV, HEAD_DIM)
    blocks = jnp.concatenate([ch[:, j:j + n_cmp] for j in range(ratio)], axis=2)
    blocks = blocks + pe[:, None, :]
    flat = blocks.transpose(0, 1, 3, 2, 4).reshape(bt, n_cmp, N_KV, CMP_LEN * HEAD_DIM)
    return jax.nn.silu(flat @ w1) @ w2


def nsa_compressed(q, full_cmp, q_off, w1, w2, pe):
    s = q.shape[1]
    kc = compress_blocks(full_cmp[:, :, 0], w1[0], w2[0], pe[0])
    vc = compress_blocks(full_cmp[:, :, 1], w1[1], w2[1], pe[1])
    n_cmp = kc.shape[1]
    q_pos = q_off + np.arange(s)
    ends = np.arange(n_cmp) * CMP_STRIDE + CMP_LEN - 1
    mask = ends[None, :] <= q_pos[:, None]
    sc = jnp.einsum('bsgrd,bngd->bsgrn', q, kc).astype(jnp.float32) * ATT_SCALE
    p = masked_softmax(sc, mask[None, :, None, None, :])
    return jnp.einsum('bsgrn,bngd->bsgrd', p, vc), p


def nsa_selected(q, full_sel, p_cmp, q_off):
    bt, s = q.shape[:2]
    t = full_sel.shape[1]
    n_slc = -(-t // SEL_LEN)
    n_cmp = p_cmp.shape[-1]
    ci = np.arange(n_cmp)[:, None]
    sj = np.arange(n_slc)[None, :]
    overlap = ((ci * CMP_STRIDE < (sj + 1) * SEL_LEN)
               & (ci * CMP_STRIDE + CMP_LEN > sj * SEL_LEN)).astype(np.float32)
    imp = jnp.einsum('bsgn,nj->bsgj', p_cmp.sum(3), overlap)
    q_pos = q_off + np.arange(s)
    q_blk = q_pos // SEL_LEN
    jj = np.arange(n_slc)[None, :]
    valid = jj * SEL_LEN <= q_pos[:, None]
    forced = (jj == 0) | (jj == q_blk[:, None]) | (jj == q_blk[:, None] - 1)
    score = jnp.where(valid[None, :, None, :],
                      imp + np.where(forced, FORCE_BONUS, 0.0).astype(np.float32)[None, :, None, :],
                      NEG_INF)
    kk = min(SEL_TOPK, n_slc)
    _, idx = lax.top_k(score, kk)
    kv = jnp.pad(full_sel, ((0, 0), (0, n_slc * SEL_LEN - t), (0, 0), (0, 0), (0, 0)))
    kv = kv.reshape(bt, n_slc, SEL_LEN, 2, N_KV, HEAD_DIM).transpose(0, 4, 1, 2, 3, 5)
    kt, vt = kv[..., 0, :], kv[..., 1, :]
    qb = _block(s, SEL_QBLOCK)
    nb = s // qb
    q_blocks = q.reshape(bt, nb, qb, N_KV, HEADS_PER_KV, HEAD_DIM).swapaxes(0, 1)
    idx_blocks = idx.reshape(bt, nb, qb, N_KV, kk).swapaxes(0, 1)
    pos_blocks = jnp.asarray(q_pos.reshape(nb, qb))
    bi = jnp.arange(bt)[:, None, None, None]
    gi = jnp.arange(N_KV)[None, None, :, None]
    offs = jnp.arange(SEL_LEN)

    def one_block(args):
        qblk, iblk, pblk = args
        kg = kt[bi, gi, iblk]
        vg = vt[bi, gi, iblk]
        kpos = iblk[..., None] * SEL_LEN + offs
        mask = (kpos <= pblk[None, :, None, None, None])[:, :, :, None]
        sc = jnp.einsum('bqgrd,bqgkld->bqgrkl', qblk, kg).astype(jnp.float32) * ATT_SCALE
        shp = sc.shape
        p = masked_softmax(sc.reshape(shp[:4] + (kk * SEL_LEN,)),
                           jnp.broadcast_to(mask, shp).reshape(shp[:4] + (kk * SEL_LEN,))).reshape(shp)
        return jnp.einsum('bqgrkl,bqgkld->bqgrd', p, vg)

    o = lax.map(one_block, (q_blocks, idx_blocks, pos_blocks))
    return o.swapaxes(0, 1).reshape(bt, s, N_KV, HEADS_PER_KV, HEAD_DIM)


def nsa_window(q, full_win, n_prev):
    bt, s = q.shape[:2]
    qb = _block(s, ATT_QBLOCK)
    nb = s // qb
    band = qb + WINDOW - 1
    kv = jnp.pad(full_win, ((0, 0), (WINDOW - 1, 0), (0, 0), (0, 0), (0, 0)))
    q_blocks = q.reshape(bt, nb, qb, N_KV, HEADS_PER_KV, HEAD_DIM).swapaxes(0, 1)

    def one_block(args):
        blk, qblk = args
        start = blk * qb + n_prev
        kvb = lax.dynamic_slice_in_dim(kv, start, band, axis=1)
        k_idx = start - (WINDOW - 1) + jnp.arange(band)
        q_idx = start + jnp.arange(qb)
        rel = q_idx[:, None] - k_idx[None, :]
        mask = (k_idx[None, :] >= 0) & (rel >= 0) & (rel < WINDOW)
        sc = jnp.einsum('bqgrd,blgd->bqgrl', qblk, kvb[:, :, 0]).astype(jnp.float32) * ATT_SCALE
        p = masked_softmax(sc, mask[None, :, None, None, :])
        return jnp.einsum('bqgrl,blgd->bqgrd', p, kvb[:, :, 1])

    o = lax.map(one_block, (jnp.arange(nb), q_blocks))
    return o.swapaxes(0, 1).reshape(bt, s, N_KV, HEADS_PER_KV, HEAD_DIM)


def _kv_rows(kv, branch):
    bt, s, _ = kv.shape
    return kv[..., 2 * branch * KV_W:(2 * branch + 2) * KV_W].reshape(bt, s, 2, N_KV, HEAD_DIM)


def nsa_sample_mix(q, kv, g, past_cmp, past_sel, win_buf, cmp_w1, cmp_w2, cmp_pos):
    bt, s, _ = q.shape
    q = q.reshape(bt, s, N_KV, HEADS_PER_KV, HEAD_DIM)
    new_cmp, new_sel, new_win = (_kv_rows(kv, br) for br in range(3))
    full_cmp = jnp.concatenate([past_cmp, new_cmp], 1)
    full_sel = jnp.concatenate([past_sel, new_sel], 1)
    full_win = jnp.concatenate([win_buf, new_win], 1)
    o_cmp, p_cmp = nsa_compressed(q, full_cmp, PAST_LEN, cmp_w1, cmp_w2, cmp_pos)
    o_sel = nsa_selected(q, full_sel, p_cmp, PAST_LEN)
    o_win = nsa_window(q, full_win, win_buf.shape[1])
    gate = jax.nn.sigmoid(g[..., :3 * N_HEADS]).reshape(bt, s, N_KV, HEADS_PER_KV, 3)
    o = gate[..., 0:1] * o_cmp + gate[..., 1:2] * o_sel + gate[..., 2:3] * o_win
    keep = min(WINDOW, full_win.shape[1])
    return o.reshape(bt, s, N_HEADS * HEAD_DIM), new_cmp, new_sel, full_win[:, full_win.shape[1] - keep:]


def _prompt_to_bsw(a):
    return a.reshape(SEQ, BATCH, -1).transpose(1, 0, 2)


def _prompt_from_bsw(a):
    return a.transpose(1, 0, 2).reshape(T_PROMPT, -1)


def _sample_to_bsw(a):
    return a.reshape(SAMPLE_GROUPS, DEC_SEQ, SUBLANE, -1).transpose(0, 2, 1, 3).reshape(DEC_BATCH, DEC_SEQ, -1)


def _sample_from_bsw(a):
    return a.reshape(SAMPLE_GROUPS, SUBLANE, DEC_SEQ, -1).transpose(0, 2, 1, 3).reshape(T_SAMPLE, -1)


def _pad_cols(w, n):
    return jnp.pad(w, ((0, 0), (0, n - w.shape[1])))


def _rope_tables():
    half = ROPE_DIM // 2
    pos = jnp.concatenate([jnp.repeat(jnp.arange(SEQ), BATCH),
                           jnp.tile(jnp.repeat(PAST_LEN + jnp.arange(DEC_SEQ), SUBLANE), SAMPLE_GROUPS)])
    inv = ROPE_THETA ** (-jnp.arange(half, dtype=F32) / half)
    ang = pos.astype(F32)[:, None] * inv
    cos, sin = jnp.cos(ang), jnp.sin(ang)
    ones = jnp.ones((T_ALL, HEAD_DIM - ROPE_DIM), F32)
    zeros = jnp.zeros((T_ALL, HEAD_DIM - ROPE_DIM), F32)
    zh = jnp.zeros((T_ALL, half), F32)
    two = lambda parts: jnp.tile(jnp.concatenate(parts, -1), (1, LANE // HEAD_DIM))
    return two([cos, cos, ones]), two([zh, sin, zeros]), two([-sin, zh, zeros])


def kernel(x_prompt, x_sample, state_s5, state_ssd, state_conv, cache_cmp_kv, cache_sel_kv, state_win_kv, page_table, norm_mix_even, w_in_even, s5_a_re, s5_a_im, s5_log_dt, s5_b_re, s5_b_im, s5_c_re, s5_c_im, s5_d, s5_glu_w, s5_glu_b, ssd_conv_w, ssd_conv_b, ssd_dt_bias, ssd_a_log, ssd_d, ssd_norm, w_out_even, norm_mix_odd, w_in_odd, cmp_w1, cmp_w2, cmp_pos, w_out_odd, norm_mlp, w_up, w_down, norm_final):
    h = jnp.concatenate([_prompt_from_bsw(x_prompt), _sample_from_bsw(x_sample)], 0)
    even_widths = (S5_WIDTH, SSD_INNER, SSD_CONV_DIM, _pad_lanes(SSD_HEADS))
    odd_widths = (N_HEADS * HEAD_DIM, 6 * KV_W, _pad_lanes(3 * N_HEADS))
    rope_tabs = _rope_tables()
    rope_blocks = (tuple(range(N_HEADS * HEAD_DIM // LANE)), (0, 2, 4), ())
    outs = {k: [] for k in ("s5_p", "s5_s", "ssd_p", "ssd_s", "conv_p", "conv_s",
                            "cmp_p", "cmp_s", "sel_p", "sel_s", "win_p", "win_s")}
    for layer in range(DEPTH):
        i = layer // 2
        if layer % 2 == 0:
            w_in = _pad_cols(w_in_even[i], sum(even_widths)).astype(BF16)
            u, z, xbc, dt_raw = norm_proj(h, norm_mix_even[i], w_in, even_widths)
            tabs = s5_tables(s5_a_re[i], s5_a_im[i], s5_log_dt[i], s5_b_re[i], s5_b_im[i], s5_c_re[i], s5_c_im[i],
                             s5_d[i], s5_glu_w[i], s5_glu_b[i])
            ya_p, s5_hp = s5_scan(u[:T_PROMPT], jnp.zeros((BATCH, 2 * S5_GROUPS * S5_STATE), F32), tabs,
                                  n_seq=1, t_len=S5_PROMPT_STEPS)
            ya_s, s5_hs = s5_scan(u[T_PROMPT:], _s5_state_to_lanes(state_s5[i]), tabs,
                                  n_seq=SAMPLE_GROUPS, t_len=DEC_SEQ)
            outs["s5_p"].append(_s5_state_from_lanes(s5_hp))
            outs["s5_s"].append(_s5_state_from_lanes(s5_hs))
            ssdw = (ssd_conv_w[i], ssd_conv_b[i], ssd_dt_bias[i], ssd_a_log[i], ssd_d[i], ssd_norm[i])
            tm = lambda a: a[:T_PROMPT].reshape(SEQ, -1)
            yb_p, ssd_h, conv_new = ssd_prompt(tm(z), tm(xbc), tm(dt_raw), *ssdw)
            outs["ssd_p"].append(ssd_h.reshape(BATCH, SSD_HEADS, SSD_HEAD_DIM, SSD_STATE))
            outs["conv_p"].append(conv_new)
            yb_s, ssd_h, conv_new = ssd_mixer(
                _sample_to_bsw(z[T_PROMPT:]), _sample_to_bsw(xbc[T_PROMPT:]),
                _sample_to_bsw(dt_raw[T_PROMPT:, :SSD_HEADS]), state_conv[i], state_ssd[i], *ssdw)
            outs["ssd_s"].append(ssd_h)
            outs["conv_s"].append(conv_new)
            yb = jnp.concatenate([yb_p.reshape(T_PROMPT, -1), _sample_from_bsw(yb_s)], 0)
            h = out_proj(h, [jnp.concatenate([ya_p, ya_s], 0), yb], w_out_even[i].astype(BF16))
        else:
            w_in = _pad_cols(w_in_odd[i], sum(odd_widths)).astype(BF16)
            q, kv, g = norm_proj(h, norm_mix_odd[i], w_in, odd_widths, rope_tabs, rope_blocks)
            kv_p = _prompt_to_bsw(kv[:T_PROMPT])
            new_cmp, new_sel, new_win = (_kv_rows(kv_p, br) for br in range(3))
            kc = compress_blocks(new_cmp[:, :, 0], cmp_w1[i][0], cmp_w2[i][0], cmp_pos[i][0])
            vc = compress_blocks(new_cmp[:, :, 1], cmp_w1[i][1], cmp_w2[i][1], cmp_pos[i][1])
            pad_c = lambda a: jnp.pad(a.reshape(BATCH, N_CMP_PROMPT, KV_W), ((0, 0), (0, LANE - N_CMP_PROMPT), (0, 0)))
            y_p = nsa_prompt(q[:T_PROMPT].reshape(SEQ, -1), g[:T_PROMPT].reshape(SEQ, -1),
                             kv[:T_PROMPT].reshape(SEQ, -1), pad_c(kc), pad_c(vc)).reshape(T_PROMPT, -1)
            outs["cmp_p"].append(new_cmp)
            outs["sel_p"].append(new_sel)
            outs["win_p"].append(new_win[:, SEQ - WINDOW:])
            past_cmp = cache_cmp_kv[i][page_table].reshape(DEC_BATCH, PAST_LEN, 2, N_KV, HEAD_DIM)
            past_sel = cache_sel_kv[i][page_table].reshape(DEC_BATCH, PAST_LEN, 2, N_KV, HEAD_DIM)
            y_s, a, b, c = nsa_sample_mix(_sample_to_bsw(q[T_PROMPT:]), _sample_to_bsw(kv[T_PROMPT:]),
                                          _sample_to_bsw(g[T_PROMPT:]), past_cmp, past_sel, state_win_kv[i],
                                          cmp_w1[i], cmp_w2[i], cmp_pos[i])
            outs["cmp_s"].append(a)
            outs["sel_s"].append(b)
            outs["win_s"].append(c)
            y = jnp.concatenate([y_p, _sample_from_bsw(y_s)], 0)
            h = out_proj(h, [y], w_out_odd[i].astype(BF16))
        h = mlp(h, norm_mlp[layer], w_up[layer].astype(BF16), w_down[layer].astype(BF16),
                norm_final, final=(layer == DEPTH - 1))
    y_prompt, y_sample = _prompt_to_bsw(h[:T_PROMPT]), _sample_to_bsw(h[T_PROMPT:])
    st = {k: jnp.stack(v) for k, v in outs.items()}
    return (y_prompt, y_sample, st["s5_p"], st["s5_s"], st["ssd_p"], st["ssd_s"], st["conv_p"], st["conv_s"],
            st["cmp_p"], st["cmp_s"], st["sel_p"], st["sel_s"], st["win_p"], st["win_s"])
```

```python
import functools

import jax
import jax.numpy as jnp
import numpy as np
from jax import lax
from jax.experimental import pallas as pl
from jax.experimental.pallas import tpu as pltpu

D_MODEL = 1024
BATCH = 8
SEQ = 2048
DEPTH = 4
DEC_BATCH = 128
DEC_SEQ = 4
PAST_LEN = 2048
PAGE_SIZE = 128

N_SSM = (DEPTH + 1) // 2
N_ATT = DEPTH // 2
NORM_EPS = 1e-5
D_FF = 4 * D_MODEL
NEG_INF = -1e30

S5_WIDTH = D_MODEL // 2
S5_GROUP = 16
S5_GROUPS = S5_WIDTH // S5_GROUP
S5_STATE = 64

SSD_INNER = D_MODEL
SSD_HEAD_DIM = 64
SSD_HEADS = SSD_INNER // SSD_HEAD_DIM
SSD_STATE = 128
SSD_GROUPS = 4
SSD_CONV = 4
SSD_CONV_DIM = SSD_INNER + 2 * SSD_GROUPS * SSD_STATE
SSD_CHUNK = 128
MIX_EVEN = S5_WIDTH + SSD_INNER
IN_EVEN = S5_WIDTH + SSD_INNER + SSD_CONV_DIM + SSD_HEADS

N_HEADS = 16
HEAD_DIM = D_MODEL // N_HEADS
N_KV = 2
HEADS_PER_KV = N_HEADS // N_KV
KV_W = N_KV * HEAD_DIM
ROPE_DIM = HEAD_DIM // 4
ROPE_THETA = 500000.0
ATT_SCALE = HEAD_DIM ** -0.5
CMP_LEN = 32
CMP_STRIDE = 16
CMP_HIDDEN = 2 * HEAD_DIM
SEL_LEN = 64
SEL_TOPK = 16
FORCE_BONUS = 1e4
WINDOW = 512
ATT_QBLOCK = 128
SEL_QBLOCK = 64
IN_ODD = N_HEADS * HEAD_DIM + 6 * KV_W + 3 * N_HEADS

T_PROMPT = BATCH * SEQ
T_SAMPLE = DEC_BATCH * DEC_SEQ
T_ALL = T_PROMPT + T_SAMPLE

LANE = 128
SUBLANE = 8
TOKEN_TILE = 512
FF_CHUNK = 1024
VMEM_LIMIT = 56 * 1024 * 1024

S5_SLAB_GROUPS = LANE // S5_GROUP
S5_SLABS = S5_GROUPS // S5_SLAB_GROUPS
S5_SLAB_STATE = S5_SLAB_GROUPS * S5_STATE
S5_PROMPT_STEPS = 256
SAMPLE_GROUPS = DEC_BATCH // SUBLANE

NSA_Q_TILE = 256
NSA_K_TILE = 256
N_SLC_PROMPT = SEQ // SEL_LEN
N_CMP_PROMPT = SEQ // CMP_STRIDE - CMP_LEN // CMP_STRIDE + 1

F32 = jnp.float32
BF16 = jnp.bfloat16


def _pad_lanes(n):
    return -(-n // LANE) * LANE


def _rms(x, g):
    return x * lax.rsqrt(jnp.mean(x * x, -1, keepdims=True) + NORM_EPS) * g


def _resident(shape):
    return pl.BlockSpec(shape, lambda *_: (0,) * len(shape), pipeline_mode=pl.Buffered(1))


def _rows(width):
    return pl.BlockSpec((TOKEN_TILE, width), lambda i: (i, 0))


def _dot(a, b):
    return jnp.dot(a, b, preferred_element_type=F32)


def _dot_nt(a, b):
    return lax.dot_general(a, b, (((1,), (1,)), ((), ())), preferred_element_type=F32)


def _split_bf16(x):
    hi = x.astype(BF16)
    return hi, (x - hi.astype(F32)).astype(BF16)


_PARAMS = pltpu.CompilerParams(dimension_semantics=("arbitrary",), vmem_limit_bytes=VMEM_LIMIT)
_PARAMS2 = pltpu.CompilerParams(dimension_semantics=("arbitrary", "arbitrary"), vmem_limit_bytes=VMEM_LIMIT)


def _rope_block(x, cos_f, sin_a, sin_b):
    return x * cos_f + pltpu.roll(x, ROPE_DIM // 2, 1) * sin_a + pltpu.roll(x, LANE - ROPE_DIM // 2, 1) * sin_b


def _norm_proj_body(x_ref, g_ref, w_ref, *refs, widths, rope_blocks):
    if rope_blocks is None:
        o_refs = refs
    else:
        cos_ref, sa_ref, sb_ref = refs[:3]
        o_refs = refs[3:]
    xn = _rms(x_ref[...], g_ref[...]).astype(BF16)
    off = 0
    for idx, (o_ref, wd) in enumerate(zip(o_refs, widths)):
        o_ref[...] = _dot(xn, w_ref[:, off:off + wd])
        off += wd
        if rope_blocks is not None:
            for blk in rope_blocks[idx]:
                cols = slice(blk * LANE, (blk + 1) * LANE)
                o_ref[:, cols] = _rope_block(o_ref[:, cols], cos_ref[...], sa_ref[...], sb_ref[...])


def norm_proj(x, g, w, widths, rope_tabs=None, rope_blocks=None):
    n = sum(widths)
    in_specs = [_rows(D_MODEL), _resident((1, D_MODEL)), _resident((D_MODEL, n))]
    args = [x, g.reshape(1, D_MODEL), w]
    if rope_blocks is not None:
        in_specs += [_rows(LANE)] * 3
        args += list(rope_tabs)
    return pl.pallas_call(
        functools.partial(_norm_proj_body, widths=widths, rope_blocks=rope_blocks),
        grid=(T_ALL // TOKEN_TILE,),
        in_specs=in_specs,
        out_specs=[_rows(wd) for wd in widths],
        out_shape=[jax.ShapeDtypeStruct((T_ALL, wd), F32) for wd in widths],
        compiler_params=_PARAMS,
        name="norm_proj",
    )(*args)


def _out_proj_body(x_ref, *refs, widths):
    y_refs, w_ref, o_ref = refs[:len(widths)], refs[len(widths)], refs[len(widths) + 1]
    acc = x_ref[...]
    off = 0
    for y_ref, wd in zip(y_refs, widths):
        acc = acc + _dot(y_ref[...].astype(BF16), w_ref[off:off + wd, :])
        off += wd
    o_ref[...] = acc


def out_proj(x, ys, w):
    widths = tuple(y.shape[1] for y in ys)
    return pl.pallas_call(
        functools.partial(_out_proj_body, widths=widths),
        grid=(T_ALL // TOKEN_TILE,),
        in_specs=[_rows(D_MODEL)] + [_rows(wd) for wd in widths] + [_resident((sum(widths), D_MODEL))],
        out_specs=_rows(D_MODEL),
        out_shape=jax.ShapeDtypeStruct((T_ALL, D_MODEL), F32),
        compiler_params=_PARAMS,
        name="out_proj",
    )(x, *ys, w)


def _mlp_body(x_ref, g_ref, wu_ref, wd_ref, gf_ref, o_ref, *, final):
    x = x_ref[...]
    xn = _rms(x, g_ref[...]).astype(BF16)
    acc = x
    for c in range(D_FF // FF_CHUNK):
        cols = slice(c * FF_CHUNK, (c + 1) * FF_CHUNK)
        h = jnp.maximum(_dot(xn, wu_ref[:, cols]), 0.0)
        acc = acc + _dot((h * h).astype(BF16), wd_ref[cols, :])
    if final:
        acc = _rms(acc, gf_ref[...])
    o_ref[...] = acc


def mlp(x, g, w_up, w_down, g_final, final):
    return pl.pallas_call(
        functools.partial(_mlp_body, final=final),
        grid=(T_ALL // TOKEN_TILE,),
        in_specs=[_rows(D_MODEL), _resident((1, D_MODEL)), _resident((D_MODEL, D_FF)),
                  _resident((D_FF, D_MODEL)), _resident((1, D_MODEL))],
        out_specs=_rows(D_MODEL),
        out_shape=jax.ShapeDtypeStruct((T_ALL, D_MODEL), F32),
        compiler_params=_PARAMS,
        name="mlp",
    )(x, g.reshape(1, D_MODEL), w_up, w_down, g_final.reshape(1, D_MODEL))


def _cmul(ar, ai, br, bi):
    return ar * br - ai * bi, ar * bi + ai * br


def s5_tables(a_re, a_im, log_dt, b_re, b_im, c_re, c_im, d_skip, glu_w, glu_b):
    dt = jnp.exp(log_dt)[:, None]
    mag = jnp.exp(a_re * dt)
    abar_re, abar_im = mag * jnp.cos(a_im * dt), mag * jnp.sin(a_im * dt)
    den = a_re * a_re + a_im * a_im
    f_re = ((abar_re - 1.0) * a_re + abar_im * a_im) / den
    f_im = (abar_im * a_re - (abar_re - 1.0) * a_im) / den
    bbar_re, bbar_im = _cmul(f_re[..., None], f_im[..., None], b_re, b_im)
    sg, ns = S5_SLAB_GROUPS, S5_SLABS
    eye = jnp.eye(sg, dtype=F32)
    a_tab = jnp.stack([abar_re.reshape(ns, S5_SLAB_STATE), abar_im.reshape(ns, S5_SLAB_STATE)], 1)

    def in_blockdiag(bb):
        return jnp.einsum('jgpc,gh->jgchp', bb.reshape(ns, sg, S5_STATE, S5_GROUP), eye).reshape(ns, LANE, S5_SLAB_STATE)

    def out_blockdiag(cc):
        return jnp.einsum('jgcp,gh->jhpgc', cc.reshape(ns, sg, S5_GROUP, S5_STATE), eye).reshape(ns, S5_SLAB_STATE, LANE)

    b_bd = jnp.concatenate([in_blockdiag(bbar_re), in_blockdiag(bbar_im)], -1).astype(BF16)
    c_bd = jnp.concatenate([out_blockdiag(c_re), out_blockdiag(-c_im)], 1).astype(BF16)
    g_bd = jnp.einsum('jgcke,gh->jgckhe', glu_w.reshape(ns, sg, S5_GROUP, 2, S5_GROUP), eye)
    g_bd = g_bd.reshape(ns, LANE, 2 * LANE).astype(BF16)
    g_b = glu_b.reshape(ns, sg, 2, S5_GROUP).transpose(0, 2, 1, 3).reshape(ns, 1, 2 * LANE)
    return a_tab, b_bd, c_bd, d_skip.reshape(ns, 1, LANE), g_bd, g_b


def _s5_body(u_ref, h0_ref, a_ref, b_ref, c_ref, d_ref, gw_ref, gb_ref, o_ref, hl_ref, st_ref, h_ref, *,
             n_seq, t_len):
    w = S5_SLAB_STATE

    @pl.when(pl.program_id(1) == 0)
    def _():
        h_ref[...] = h0_ref[...]

    ub = u_ref[...]
    st_ref[...] = _dot(ub.astype(BF16), b_ref[0])
    ar = jnp.broadcast_to(a_ref[0, 0:1, :], (SUBLANE, w))
    ai = jnp.broadcast_to(a_ref[0, 1:2, :], (SUBLANE, w))
    for s in range(n_seq):
        base = s * t_len * SUBLANE
        rows = slice(s * SUBLANE, (s + 1) * SUBLANE)

        def step(t, carry, base=base):
            hr, hi = carry
            r = pl.ds(pl.multiple_of(base + t * SUBLANE, SUBLANE), SUBLANE)
            nhr = ar * hr - ai * hi + st_ref[r, 0:w]
            nhi = ar * hi + ai * hr + st_ref[r, w:2 * w]
            st_ref[r, 0:w] = nhr
            st_ref[r, w:2 * w] = nhi
            return nhr, nhi

        hr, hi = lax.fori_loop(0, t_len, step, (h_ref[rows, 0:w], h_ref[rows, w:2 * w]), unroll=min(t_len, 8))
        h_ref[rows, 0:w] = hr
        h_ref[rows, w:2 * w] = hi
    y = _dot(st_ref[...].astype(BF16), c_ref[0]) + d_ref[0] * ub
    zg = _dot(y.astype(BF16), gw_ref[0]) + gb_ref[0]
    o_ref[...] = zg[:, :LANE] * jax.nn.sigmoid(zg[:, LANE:])
    hl_ref[...] = h_ref[...]


def s5_scan(u, h0, tabs, *, n_seq, t_len):
    blk = n_seq * t_len * SUBLANE
    n_chunks = u.shape[0] // blk
    assert n_chunks * blk == u.shape[0] and (n_seq == 1 or n_chunks == 1)
    hb = n_seq * SUBLANE
    slab = lambda shape: pl.BlockSpec((1,) + shape, lambda j, c: (j, 0, 0))
    return pl.pallas_call(
        functools.partial(_s5_body, n_seq=n_seq, t_len=t_len),
        grid=(S5_SLABS, n_chunks),
        in_specs=[pl.BlockSpec((blk, LANE), lambda j, c: (c, j)),
                  pl.BlockSpec((hb, 2 * S5_SLAB_STATE), lambda j, c: (0, j)),
                  slab((2, S5_SLAB_STATE)), slab((LANE, 2 * S5_SLAB_STATE)), slab((2 * S5_SLAB_STATE, LANE)),
                  slab((1, LANE)), slab((LANE, 2 * LANE)), slab((1, 2 * LANE))],
        out_specs=[pl.BlockSpec((blk, LANE), lambda j, c: (c, j)),
                   pl.BlockSpec((hb, 2 * S5_SLAB_STATE), lambda j, c: (0, j))],
        out_shape=[jax.ShapeDtypeStruct(u.shape, F32), jax.ShapeDtypeStruct(h0.shape, F32)],
        scratch_shapes=[pltpu.VMEM((blk, 2 * S5_SLAB_STATE), F32), pltpu.VMEM((hb, 2 * S5_SLAB_STATE), F32)],
        compiler_params=_PARAMS2,
        name="s5_scan",
    )(u, h0, *tabs)


def _s5_state_to_lanes(h):
    bt = h.shape[0]
    return h.reshape(bt, S5_SLABS, S5_SLAB_STATE, 2).transpose(0, 1, 3, 2).reshape(bt, -1)


def _s5_state_from_lanes(h):
    bt = h.shape[0]
    return h.reshape(bt, S5_SLABS, 2, S5_SLAB_STATE).transpose(0, 1, 3, 2).reshape(bt, S5_GROUPS, S5_STATE, 2)


SSD_BC = SSD_GROUPS * SSD_STATE
SSD_GROUP_W = SSD_INNER // SSD_GROUPS
CONV_PAD = SUBLANE


def _ssd_consts():
    tri = np.tril(np.ones((SSD_CHUNK, SSD_CHUNK), np.float32))
    expand = np.zeros((LANE, SSD_INNER), np.float32)
    for h in range(SSD_HEADS):
        expand[h, h * SSD_HEAD_DIM:(h + 1) * SSD_HEAD_DIM] = 1.0
    return jnp.asarray(tri, BF16), jnp.asarray(expand, BF16)


def _split3_dot(m, x):
    hi, r = _split_bf16(x)
    x2 = x - hi.astype(F32) - r.astype(F32)
    return _dot(m, hi) + _dot(m, r) + _dot(m, x2.astype(BF16))


def _expand_heads(x, eh):
    hi, lo = _split_bf16(x)
    return _dot(hi, eh) + _dot(lo, eh)


def _ssd_prompt_body(z_ref, x_ref, dt_ref, cw_ref, cb_ref, dtb_ref, alog_ref, d_ref, ng_ref, tri_ref, eh_ref,
                     y_ref, hl_ref, cl_ref, xs_ref, h_ref):
    n = SSD_CHUNK
    k0 = CONV_PAD - (SSD_CONV - 1)

    @pl.when(pl.program_id(1) == 0)
    def _():
        h_ref[...] = jnp.zeros(h_ref.shape, F32)
        xs_ref[0:CONV_PAD, :] = jnp.zeros((CONV_PAD, SSD_CONV_DIM), F32)

    xs_ref[CONV_PAD:CONV_PAD + n, :] = x_ref[...]
    conv = cb_ref[...]
    for k in range(SSD_CONV):
        conv = conv + xs_ref[k0 + k:k0 + k + n, :] * cw_ref[k:k + 1, :]
    tail = xs_ref[k0 + n:CONV_PAD + n, :]
    xs_ref[k0:CONV_PAD, :] = tail
    cl_ref[0] = tail
    xa = conv * jax.nn.sigmoid(conv)
    x = xa[:, :SSD_INNER]

    dt = jax.nn.softplus(dt_ref[...] + dtb_ref[...])
    a_dt = dt * (-jnp.exp(alog_ref[...]))
    a_cs = _split3_dot(tri_ref[...], a_dt)
    a_cs_t = a_cs.T
    dt_t = dt.T
    a_tot = a_cs[n - 1:n, :]
    eh = eh_ref[...]
    decay_in = _expand_heads(jnp.exp(a_cs), eh)
    xw = x * _expand_heads(jnp.exp(a_tot - a_cs) * dt, eh)
    lower = (lax.broadcasted_iota(jnp.int32, (n, n), 0) >= lax.broadcasted_iota(jnp.int32, (n, n), 1))
    low = lax.broadcasted_iota(jnp.int32, (n, LANE), 1) < SSD_HEAD_DIM
    heads_per_group = SSD_HEADS // SSD_GROUPS
    for g in range(SSD_GROUPS):
        b_g = xa[:, SSD_INNER + g * SSD_STATE:SSD_INNER + (g + 1) * SSD_STATE].astype(BF16)
        c_g = xa[:, SSD_INNER + SSD_BC + g * SSD_STATE:SSD_INNER + SSD_BC + (g + 1) * SSD_STATE].astype(BF16)
        gcols = slice(g * SSD_GROUP_W, (g + 1) * SSD_GROUP_W)
        h_g = h_ref[gcols, :]
        cb = _dot_nt(c_g, b_g)
        y_off = _dot_nt(c_g, h_g.astype(BF16)) * decay_in[:, gcols]
        for pr in range(heads_per_group // 2):
            cols = slice(g * SSD_GROUP_W + pr * LANE, g * SSD_GROUP_W + (pr + 1) * LANE)
            x_pair = x[:, cols]
            y_pair = y_off[:, pr * LANE:(pr + 1) * LANE] + x_pair * d_ref[:, cols]
            for side in range(2):
                h = g * heads_per_group + 2 * pr + side
                seg = jnp.where(lower, jnp.exp(a_cs[:, h:h + 1] - a_cs_t[h:h + 1, :]), 0.0)
                m = (cb * seg * dt_t[h:h + 1, :]).astype(BF16)
                x_side = jnp.where(low if side == 0 else ~low, x_pair, 0.0).astype(BF16)
                y_pair = y_pair + _dot(m, x_side)
            y_ref[:, cols] = y_pair
        upd = _dot(xw[:, gcols].T.astype(BF16), b_g)
        for hh in range(heads_per_group):
            h = g * heads_per_group + hh
            rows = slice(g * SSD_GROUP_W + hh * SSD_HEAD_DIM, g * SSD_GROUP_W + (hh + 1) * SSD_HEAD_DIM)
            scale = jnp.exp(jnp.broadcast_to(a_tot[:, h:h + 1], (SSD_HEAD_DIM, SSD_STATE)))
            h_ref[rows, :] = h_ref[rows, :] * scale + upd[hh * SSD_HEAD_DIM:(hh + 1) * SSD_HEAD_DIM, :]
    z = z_ref[...]
    y_ref[...] = _rms(y_ref[...] * (z * jax.nn.sigmoid(z)), ng_ref[...])
    hl_ref[0] = h_ref[...]


def ssd_prompt(z, xbc, dt_raw, conv_w, conv_b, dt_bias, a_log, d_skip, norm_g):
    n = SSD_CHUNK
    nb = z.shape[1] // SSD_INNER
    tri, eh = _ssd_consts()
    lanes = lambda v: jnp.pad(v, (0, LANE - v.shape[0])).reshape(1, LANE)
    chunk = lambda w: pl.BlockSpec((n, w), lambda b, c: (c, b))
    return pl.pallas_call(
        _ssd_prompt_body,
        grid=(nb, SEQ // n),
        in_specs=[chunk(SSD_INNER), chunk(SSD_CONV_DIM), chunk(LANE),
                  _resident((SSD_CONV, SSD_CONV_DIM)), _resident((1, SSD_CONV_DIM)), _resident((1, LANE)),
                  _resident((1, LANE)), _resident((1, SSD_INNER)), _resident((1, SSD_INNER)),
                  _resident(tri.shape), _resident(eh.shape)],
        out_specs=[chunk(SSD_INNER),
                   pl.BlockSpec((1, SSD_INNER, SSD_STATE), lambda b, c: (b, 0, 0)),
                   pl.BlockSpec((1, SSD_CONV - 1, SSD_CONV_DIM), lambda b, c: (b, 0, 0))],
        out_shape=[jax.ShapeDtypeStruct((SEQ, nb * SSD_INNER), F32),
                   jax.ShapeDtypeStruct((nb, SSD_INNER, SSD_STATE), F32),
                   jax.ShapeDtypeStruct((nb, SSD_CONV - 1, SSD_CONV_DIM), F32)],
        scratch_shapes=[pltpu.VMEM((CONV_PAD + n, SSD_CONV_DIM), F32), pltpu.VMEM((SSD_INNER, SSD_STATE), F32)],
        compiler_params=_PARAMS2,
        name="ssd_prompt",
    )(z, xbc, dt_raw, conv_w, conv_b.reshape(1, -1), lanes(dt_bias), lanes(a_log),
      jnp.repeat(d_skip, SSD_HEAD_DIM).reshape(1, -1), norm_g.reshape(1, -1), tri, eh)


def _nsa_consts():
    nq = N_SLC_PROMPT
    ci = np.arange(LANE)[None, :]
    sj = np.arange(nq)[:, None]
    overlap_t = ((ci * CMP_STRIDE < (sj + 1) * SEL_LEN) & (ci * CMP_STRIDE + CMP_LEN > sj * SEL_LEN)
                 & (ci < N_CMP_PROMPT)).astype(np.float32)
    keys = np.arange(SEQ)
    expand = (keys[:, None] // SEL_LEN == np.arange(nq)[None, :]).astype(np.float32)
    expand = expand.reshape(SEQ // NSA_K_TILE, NSA_K_TILE, nq)
    gate = np.zeros((3, N_HEADS * HEAD_DIM, LANE), np.float32)
    for br in range(3):
        for h in range(N_HEADS):
            gate[br, h * HEAD_DIM:(h + 1) * HEAD_DIM, h * 3 + br] = 1.0
    return jnp.asarray(overlap_t, BF16), jnp.asarray(expand, BF16), jnp.asarray(gate, BF16)


def _k_variants(k):
    low = lax.broadcasted_iota(jnp.int32, k.shape, 1) < HEAD_DIM
    k = k * ATT_SCALE
    k_sw = pltpu.roll(k, HEAD_DIM, 1)
    kl = (jnp.where(low, k, 0.0).astype(BF16), jnp.where(low, k_sw, 0.0).astype(BF16))
    kr = (jnp.where(low, 0.0, k_sw).astype(BF16), jnp.where(low, 0.0, k).astype(BF16))
    return kl, kr


def _vt_variants(v):
    vt = v.T
    zero = jnp.zeros((HEAD_DIM, v.shape[0]), F32)
    top, bot = vt[:HEAD_DIM], vt[HEAD_DIM:]
    vl = (jnp.concatenate([top, zero], 0).astype(BF16), jnp.concatenate([bot, zero], 0).astype(BF16))
    vr = (jnp.concatenate([zero, top], 0).astype(BF16), jnp.concatenate([zero, bot], 0).astype(BF16))
    return vl, vr


def _nsa_prompt_body(q_ref, g_ref, kv_ref, kc_ref, vc_ref, ov_ref, ex_ref, eg_ref, o_ref,
                     kl_ref, kr_ref, vl_ref, vr_ref, acc_ref, m_ref, l_ref, ob_ref):
    tq = NSA_Q_TILE
    tk = NSA_K_TILE
    n_pairs = HEADS_PER_KV // 2
    qi = pl.program_id(1)
    q0 = qi * tq

    @pl.when(qi == 0)
    def _():
        for br in range(2):
            c0 = (2 + 2 * br) * LANE
            kl, kr = _k_variants(kv_ref[:, c0:c0 + LANE])
            for g in range(N_KV):
                kl_ref[br, g] = kl[g]
                kr_ref[br, g] = kr[g]
            for t in range(SEQ // tk):
                vl, vr = _vt_variants(kv_ref[t * tk:(t + 1) * tk, c0 + LANE:c0 + 2 * LANE])
                for g in range(N_KV):
                    vl_ref[br, g, t] = vl[g]
                    vr_ref[br, g, t] = vr[g]

    row_low = lax.broadcasted_iota(jnp.int32, (LANE, tq), 0) < HEAD_DIM
    kpos_t = lax.broadcasted_iota(jnp.int32, (tk, tq), 0)
    qpos_t = q0 + lax.broadcasted_iota(jnp.int32, (tk, tq), 1)
    kcl, kcr = _k_variants(kc_ref[0])
    vcl, vcr = _vt_variants(vc_ref[0])
    n_iota = lax.broadcasted_iota(jnp.int32, (LANE, tq), 0)
    cmp_mask = ((n_iota * CMP_STRIDE + (CMP_LEN - 1) <= q0 + lax.broadcasted_iota(jnp.int32, (LANE, tq), 1))
                & (n_iota < N_CMP_PROMPT))

    def pair_q(g, pr):
        c0 = (g * n_pairs + pr) * LANE
        return q_ref[:, c0:c0 + LANE].astype(BF16)

    def flash(br, g, n_tiles, tile0, mask_fn, reverse):
        m_ref[...] = jnp.full(m_ref.shape, NEG_INF, F32)
        l_ref[...] = jnp.zeros(l_ref.shape, F32)
        acc_ref[...] = jnp.zeros(acc_ref.shape, F32)

        def tile(kt, carry):
            if reverse:
                kt = n_tiles - 1 - kt
            kt = tile0 + kt
            k0 = pl.multiple_of(kt * tk, tk)
            mask = mask_fn(kt, k0)
            for pr in range(n_pairs):
                qp = pair_q(g, pr)
                ps = []
                alpha = []
                for side, k_ref in enumerate((kl_ref, kr_ref)):
                    hh = 2 * pr + side
                    s = jnp.where(mask, _dot_nt(k_ref[br, g, pl.ds(k0, tk), :], qp), NEG_INF)
                    m_old = m_ref[hh:hh + 1, :]
                    m_new = jnp.maximum(m_old, jnp.max(s, 0, keepdims=True))
                    p = jnp.exp(s - m_new)
                    a = jnp.exp(m_old - m_new)
                    l_ref[hh:hh + 1, :] = a * l_ref[hh:hh + 1, :] + jnp.sum(p, 0, keepdims=True)
                    m_ref[hh:hh + 1, :] = m_new
                    ps.append(p.astype(BF16))
                    alpha.append(a)
                pv = _dot(vl_ref[br, g, kt], ps[0]) + _dot(vr_ref[br, g, kt], ps[1])
                acc_ref[pr] = acc_ref[pr] * jnp.where(row_low, alpha[0], alpha[1]) + pv
            return carry

        lax.fori_loop(0, n_tiles, tile, 0)
        for pr in range(n_pairs):
            r0 = (g * n_pairs + pr) * LANE
            l_pair = jnp.where(row_low, l_ref[2 * pr:2 * pr + 1, :], l_ref[2 * pr + 1:2 * pr + 2, :])
            ob_ref[br + 1, r0:r0 + LANE, :] = acc_ref[pr] / l_pair

    for g in range(N_KV):
        psum = jnp.zeros((LANE, tq), F32)
        for pr in range(n_pairs):
            qp = pair_q(g, pr)
            ps = []
            for kc in (kcl[g], kcr[g]):
                s = jnp.where(cmp_mask, _dot_nt(kc, qp), NEG_INF)
                p = jnp.where(cmp_mask, jnp.exp(s - jnp.max(s, 0, keepdims=True)), 0.0)
                p = p / jnp.maximum(jnp.sum(p, 0, keepdims=True), 1e-30)
                psum = psum + p
                ps.append(p.astype(BF16))
            r0 = (g * n_pairs + pr) * LANE
            ob_ref[0, r0:r0 + LANE, :] = _dot(vcl[g], ps[0]) + _dot(vcr[g], ps[1])

        p_hi, p_lo = _split_bf16(psum)
        imp = _dot(ov_ref[...], p_hi) + _dot(ov_ref[...], p_lo)
        shp = (N_SLC_PROMPT, tq)
        jj = lax.broadcasted_iota(jnp.int32, shp, 0)
        qp_t = q0 + lax.broadcasted_iota(jnp.int32, shp, 1)
        qb_t = qp_t // SEL_LEN
        forced = (jj == 0) | (jj == qb_t) | (jj == qb_t - 1)
        score = jnp.where(jj * SEL_LEN <= qp_t, imp + jnp.where(forced, FORCE_BONUS, 0.0), NEG_INF)
        rank = jnp.zeros(shp, F32)
        for i in range(N_SLC_PROMPT):
            row = score[i:i + 1, :]
            ahead = (row > score) | ((row == score) & (jj > i))
            rank = rank + jnp.where(ahead, 1.0, 0.0)
        sel_t = jnp.where(rank < SEL_TOPK, 1.0, 0.0).astype(BF16)

        def sel_mask(kt, k0, sel_t=sel_t):
            chosen = _dot(ex_ref[kt], sel_t) > 0.5
            return chosen & (k0 + kpos_t <= qpos_t)

        flash(0, g, (q0 + tq) // tk, 0, sel_mask, reverse=False)

        def win_mask(kt, k0):
            rel = qpos_t - (k0 + kpos_t)
            return (rel >= 0) & (rel < WINDOW)

        w0 = jnp.maximum(q0 - WINDOW, 0) // tk
        flash(1, g, (q0 + tq) // tk - w0, w0, win_mask, reverse=True)

    g_hi, g_lo = _split_bf16(jax.nn.sigmoid(g_ref[...]))
    out_t = jnp.zeros((N_HEADS * HEAD_DIM, tq), F32)
    for br in range(3):
        out_t = out_t + (_dot_nt(eg_ref[br], g_hi) + _dot_nt(eg_ref[br], g_lo)) * ob_ref[br]
    for c in range(N_HEADS * HEAD_DIM // LANE):
        o_ref[:, c * LANE:(c + 1) * LANE] = out_t[c * LANE:(c + 1) * LANE, :].T


def nsa_prompt(q, gate, kv, kc, vc):
    tq = NSA_Q_TILE
    qw = N_HEADS * HEAD_DIM
    ov, ex, eg = _nsa_consts()
    return pl.pallas_call(
        _nsa_prompt_body,
        grid=(kc.shape[0], SEQ // tq),
        in_specs=[pl.BlockSpec((tq, qw), lambda b, i: (i, b)),
                  pl.BlockSpec((tq, LANE), lambda b, i: (i, b)),
                  pl.BlockSpec((SEQ, 6 * KV_W), lambda b, i: (0, b)),
                  pl.BlockSpec((1, LANE, LANE), lambda b, i: (b, 0, 0)),
                  pl.BlockSpec((1, LANE, LANE), lambda b, i: (b, 0, 0)),
                  _resident(ov.shape), _resident(ex.shape), _resident(eg.shape)],
        out_specs=pl.BlockSpec((tq, qw), lambda b, i: (i, b)),
        out_shape=jax.ShapeDtypeStruct((SEQ, kc.shape[0] * qw), F32),
        scratch_shapes=[pltpu.VMEM((2, N_KV, SEQ, LANE), BF16), pltpu.VMEM((2, N_KV, SEQ, LANE), BF16),
                        pltpu.VMEM((2, N_KV, SEQ // NSA_K_TILE, LANE, NSA_K_TILE), BF16),
                        pltpu.VMEM((2, N_KV, SEQ // NSA_K_TILE, LANE, NSA_K_TILE), BF16),
                        pltpu.VMEM((HEADS_PER_KV // 2, LANE, tq), F32),
                        pltpu.VMEM((HEADS_PER_KV, tq), F32), pltpu.VMEM((HEADS_PER_KV, tq), F32),
                        pltpu.VMEM((3, qw, tq), F32)],
        compiler_params=_PARAMS2,
        name="nsa_prompt",
    )(q, gate, kv, kc, vc, ov, ex, eg)


N_PAGES = PAST_LEN // PAGE_SIZE
PAGE_CHUNKS = PAGE_SIZE // CMP_STRIDE
CHUNK_W = CMP_STRIDE * 2 * KV_W
N_CMP_SAMPLE = (PAST_LEN + DEC_SEQ) // CMP_STRIDE - CMP_LEN // CMP_STRIDE + 1
N_SLC_SAMPLE = -(-(PAST_LEN + DEC_SEQ) // SEL_LEN)
Q_ROWS = N_KV * HEADS_PER_KV * DEC_SEQ
NEW_PAD = LANE


def _nsa_sample_consts():
    ni = np.arange(LANE)[:, None]
    sj = np.arange(LANE)[None, :]
    overlap = ((ni * CMP_STRIDE < (sj + 1) * SEL_LEN) & (ni * CMP_STRIDE + CMP_LEN > sj * SEL_LEN)
               & (ni < N_CMP_SAMPLE) & (sj < N_SLC_SAMPLE)).astype(np.float32)
    keys = np.arange(PAST_LEN)[None, :]
    expand = (keys // SEL_LEN == np.arange(LANE)[:, None]).astype(np.float32)
    r = np.arange(Q_ROWS)
    same = ((r[:, None] // (HEADS_PER_KV * DEC_SEQ) == r[None, :] // (HEADS_PER_KV * DEC_SEQ))
            & (r[:, None] % DEC_SEQ == r[None, :] % DEC_SEQ)).astype(np.float32)
    return jnp.asarray(overlap, BF16), jnp.asarray(expand, BF16), jnp.asarray(same, BF16)


def _cmp_weights(w1, w2, pe):
    half = CMP_LEN // CMP_STRIDE
    eye = jnp.eye(N_KV, dtype=F32)
    w1r = w1.reshape(half, CMP_STRIDE, HEAD_DIM, CMP_HIDDEN)
    bd = jnp.einsum('aldh,gk->algdkh', w1r, eye).reshape(half, CMP_STRIDE * KV_W, N_KV * CMP_HIDDEN)
    w1_bd = jnp.concatenate([bd[j] for j in range(half)], 1).astype(BF16)
    c1 = jnp.dot(pe.reshape(1, -1), w1, precision=lax.Precision.HIGHEST)
    w2_bd = jnp.einsum('hd,gk->ghkd', w2, eye).reshape(N_KV * CMP_HIDDEN, KV_W).astype(BF16)
    return w1_bd, jnp.tile(c1, (1, N_KV)), w2_bd


def _softmax_rows(parts, masks):
    s = [jnp.where(m, p, NEG_INF) for p, m in zip(parts, masks)]
    top = functools.reduce(jnp.maximum, [jnp.max(x, -1, keepdims=True) for x in s])
    e = [jnp.where(m, jnp.exp(x - top), 0.0) for x, m in zip(s, masks)]
    den = functools.reduce(lambda a, b: a + b, [jnp.sum(x, -1, keepdims=True) for x in e])
    return [x / jnp.maximum(den, 1e-30) for x in e]


def _nsa_sample_body(pt_ref, *refs):
    cmp_refs = refs[:N_PAGES]
    sel_refs = refs[N_PAGES:2 * N_PAGES]
    (win_ref, q_ref, new_ref, gate_ref, w1k_ref, c1k_ref, w2k_ref, w1v_ref, c1v_ref, w2v_ref,
     ov_ref, ex_ref, same_ref, o_ref) = refs[2 * N_PAGES:]
    del pt_ref
    nblk = N_PAGES * PAGE_CHUNKS
    hw = N_KV * CMP_HIDDEN

    chunks = jnp.concatenate([r[0] for r in cmp_refs], 0)
    tiles = CHUNK_W // LANE

    def compress(kv, w1_ref, c1_ref, w2_ref):
        a = jnp.concatenate([chunks[:, t * LANE:(t + 1) * LANE] for t in range(kv, tiles, 2)], 1).astype(BF16)
        pq = _dot(a, w1_ref[...])
        h1 = pq[:, :hw] + pltpu.roll(pq[:, hw:], nblk - 1, 0) + c1_ref[...]
        return _dot((h1 * jax.nn.sigmoid(h1)).astype(BF16), w2_ref[...]).astype(BF16)

    kc = compress(0, w1k_ref, c1k_ref, w2k_ref)
    vc = compress(1, w1v_ref, c1v_ref, w2v_ref)

    q = q_ref[0].astype(BF16)
    lane = lax.broadcasted_iota(jnp.int32, (Q_ROWS, LANE), 1)
    t_row = lax.broadcasted_iota(jnp.int32, (Q_ROWS, LANE), 0) % DEC_SEQ
    (p_cmp,) = _softmax_rows([_dot_nt(q, kc)], [lane < N_CMP_SAMPLE])
    o_cmp = _dot(p_cmp.astype(BF16), vc)

    p_hi, p_lo = _split_bf16(p_cmp)
    g_hi, g_lo = _split_bf16(_dot(same_ref[...], p_hi) + _dot(same_ref[...], p_lo))
    imp = _dot(g_hi, ov_ref[...]) + _dot(g_lo, ov_ref[...])
    q_blk = (PAST_LEN + t_row) // SEL_LEN
    forced = (lane == 0) | (lane == q_blk) | (lane == q_blk - 1)
    valid = (lane * SEL_LEN <= PAST_LEN + t_row) & (lane < N_SLC_SAMPLE)
    score = jnp.where(valid, imp + jnp.where(forced, FORCE_BONUS, 0.0), NEG_INF)
    rank = jnp.zeros((Q_ROWS, LANE), F32)
    for i in range(N_SLC_SAMPLE):
        col = score[:, i:i + 1]
        ahead = (col > score) | ((col == score) & (lane > i))
        rank = rank + jnp.where(ahead, 1.0, 0.0)
    sel = jnp.where((rank < SEL_TOPK) & (lane < N_SLC_SAMPLE), 1.0, 0.0).astype(BF16)
    chosen = _dot(sel, ex_ref[...]) > 0.5

    new = new_ref[0]
    pad_new = lambda c0: jnp.concatenate([new[:, c0:c0 + LANE], jnp.zeros((NEW_PAD - new.shape[0], LANE), F32)],
                                         0).astype(BF16)
    new_mask = (lane <= t_row) & (lane < DEC_SEQ)

    k_sel = jnp.concatenate([r[0][:, 0:LANE] for r in sel_refs], 0).astype(BF16)
    v_sel = jnp.concatenate([r[0][:, LANE:2 * LANE] for r in sel_refs], 0).astype(BF16)
    p_past, p_new = _softmax_rows([_dot_nt(q, k_sel), _dot_nt(q, pad_new(2 * LANE))], [chosen, new_mask])
    o_sel = _dot(p_past.astype(BF16), v_sel) + _dot(p_new.astype(BF16), pad_new(3 * LANE))

    win = win_ref[0]
    r_iota = lax.broadcasted_iota(jnp.int32, (Q_ROWS, WINDOW), 1)
    t_win = lax.broadcasted_iota(jnp.int32, (Q_ROWS, WINDOW), 0) % DEC_SEQ
    p_buf, p_new = _softmax_rows([_dot_nt(q, win[:, 0:LANE].astype(BF16)), _dot_nt(q, pad_new(4 * LANE))],
                                 [r_iota > t_win, new_mask])
    o_win = _dot(p_buf.astype(BF16), win[:, LANE:2 * LANE].astype(BF16)) + _dot(p_new.astype(BF16), pad_new(5 * LANE))

    gate = jax.nn.sigmoid(gate_ref[0])
    o_ref[0] = gate[:, 0:1] * o_cmp + gate[:, 1:2] * o_sel + gate[:, 2:3] * o_win


def nsa_sample(layer, page_table, cache_cmp, cache_sel, win_buf, q_rows, new_kv, gate_rows, cmp_w1, cmp_w2, cmp_pos):
    n_pool = cache_cmp.shape[1]
    off = layer * n_pool
    cmp_view = cache_cmp.reshape(-1, PAGE_CHUNKS, CHUNK_W)
    sel_view = cache_sel.reshape(-1, PAGE_SIZE, 2 * KV_W)
    win_view = win_buf.reshape(-1, WINDOW, 2 * KV_W)
    wk = _cmp_weights(cmp_w1[0], cmp_w2[0], cmp_pos[0])
    wv = _cmp_weights(cmp_w1[1], cmp_w2[1], cmp_pos[1])
    consts = _nsa_sample_consts()
    page = lambda shape, p: pl.BlockSpec((1,) + shape, lambda b, pt: (off + pt[b, p], 0, 0))
    per_b = lambda shape: pl.BlockSpec((1,) + shape, lambda b, pt: (b, 0, 0))
    const = lambda a: pl.BlockSpec(a.shape, lambda b, pt: (0,) * a.ndim, pipeline_mode=pl.Buffered(1))
    weights = list(wk) + list(wv) + list(consts)
    in_specs = ([page((PAGE_CHUNKS, CHUNK_W), p) for p in range(N_PAGES)]
                + [page((PAGE_SIZE, 2 * KV_W), p) for p in range(N_PAGES)]
                + [pl.BlockSpec((1, WINDOW, 2 * KV_W), lambda b, pt: (layer * win_buf.shape[1] + b, 0, 0)),
                   per_b((Q_ROWS, LANE)), per_b(new_kv.shape[1:]), per_b((Q_ROWS, LANE))]
                + [const(a) for a in weights])
    return pl.pallas_call(
        _nsa_sample_body,
        grid_spec=pltpu.PrefetchScalarGridSpec(
            num_scalar_prefetch=1, grid=(q_rows.shape[0],), in_specs=in_specs,
            out_specs=pl.BlockSpec((1, Q_ROWS, LANE), lambda b, pt: (b, 0, 0))),
        out_shape=jax.ShapeDtypeStruct((q_rows.shape[0], Q_ROWS, LANE), F32),
        compiler_params=_PARAMS,
        name="nsa_sample",
    )(page_table, *([cmp_view] * N_PAGES), *([sel_view] * N_PAGES), win_view, q_rows, new_kv, gate_rows, *weights)


def nsa_sample_layer(layer, q, kv, g, page_table, cache_cmp, cache_sel, win_buf, cmp_w1, cmp_w2, cmp_pos):
    nb = q.shape[0]
    eye = jnp.eye(N_KV, dtype=F32)
    by_head = lambda a, w: (a.reshape(nb, DEC_SEQ, N_KV, HEADS_PER_KV, w)
                            .transpose(0, 2, 3, 1, 4).reshape(nb, N_KV, HEADS_PER_KV * DEC_SEQ, w))
    q_s = by_head(q * ATT_SCALE, HEAD_DIM)
    q_rows = jnp.stack([q_s * eye[:, k][None, :, None, None] for k in range(N_KV)], 3).reshape(nb, Q_ROWS, LANE)
    g_s = by_head(g[..., :3 * N_HEADS], 3).reshape(nb, Q_ROWS, 3)
    gate_rows = jnp.pad(g_s, ((0, 0), (0, 0), (0, LANE - 3)))
    o = nsa_sample(layer, page_table, cache_cmp, cache_sel, win_buf, q_rows,
                   jnp.pad(kv, ((0, 0), (0, SUBLANE - DEC_SEQ), (0, 0))), gate_rows, cmp_w1, cmp_w2, cmp_pos)
    o = o.reshape(nb, N_KV, HEADS_PER_KV * DEC_SEQ, N_KV, HEAD_DIM)
    o = jnp.stack([o[:, k, :, k, :] for k in range(N_KV)], 1)
    return (o.reshape(nb, N_KV, HEADS_PER_KV, DEC_SEQ, HEAD_DIM).transpose(0, 3, 1, 2, 4)
            .reshape(nb, DEC_SEQ, N_HEADS * HEAD_DIM))


def masked_softmax(s, mask):
    p = jax.nn.softmax(jnp.where(mask, s, NEG_INF), axis=-1)
    return jnp.where(mask, p, 0.0)


def _segsum_exp(a):
    t = a.shape[-1]
    cs = jnp.cumsum(a, -1)
    tril = np.tril(np.ones((t, t), dtype=bool))
    return jnp.exp(jnp.where(tril, cs[..., :, None] - cs[..., None, :], -jnp.inf))


def _block(n, pref):
    return pref if n % pref == 0 else n


def ssd_mixer(z, xbc, dt_raw, conv_buf, h0, conv_w, conv_b, dt_bias, a_log, d_skip, norm_g):
    bt, s, _ = xbc.shape
    xpad = jnp.concatenate([conv_buf, xbc], 1)
    conv = conv_b + sum(xpad[:, k:k + s] * conv_w[k] for k in range(SSD_CONV))
    new_buf = xpad[:, s:]
    xbc_a = jax.nn.silu(conv)
    n_bc = SSD_GROUPS * SSD_STATE
    x = xbc_a[..., :SSD_INNER].reshape(bt, s, SSD_HEADS, SSD_HEAD_DIM)
    bm = xbc_a[..., SSD_INNER:SSD_INNER + n_bc]
    cm = xbc_a[..., SSD_INNER + n_bc:]
    dt = jax.nn.softplus(dt_raw + dt_bias)
    a = -jnp.exp(a_log)
    q = _block(s, SSD_CHUNK)
    nc = s // q
    r = SSD_HEADS // SSD_GROUPS
    xdt = (x * dt[..., None]).reshape(bt, nc, q, SSD_GROUPS, r, SSD_HEAD_DIM)
    bm = bm.reshape(bt, nc, q, SSD_GROUPS, SSD_STATE)
    cm = cm.reshape(bt, nc, q, SSD_GROUPS, SSD_STATE)
    a_dt = (dt * a).reshape(bt, nc, q, SSD_GROUPS, r).transpose(0, 3, 4, 1, 2)
    a_cs = jnp.cumsum(a_dt, -1)
    lmat = _segsum_exp(a_dt)
    cb = jnp.einsum('bclgn,bcsgn->bcgls', cm, bm)
    y_diag = jnp.einsum('bcgls,bgrcls,bcsgrp->bclgrp', cb, lmat, xdt)
    decay = jnp.exp(a_cs[..., -1:] - a_cs)
    states = jnp.einsum('bclgn,bgrcl,bclgrp->bcgrpn', bm, decay, xdt)
    h0g = h0.reshape(bt, 1, SSD_GROUPS, r, SSD_HEAD_DIM, SSD_STATE)
    states = jnp.concatenate([h0g, states], 1)
    chunk_decay = _segsum_exp(jnp.pad(a_cs[..., -1], [(0, 0)] * 3 + [(1, 0)]))
    states = jnp.einsum('bgrzc,bcgrpn->bzgrpn', chunk_decay, states)
    y_off = jnp.einsum('bclgn,bcgrpn,bgrcl->bclgrp', cm, states[:, :-1], jnp.exp(a_cs))
    y = (y_diag + y_off).reshape(bt, s, SSD_HEADS, SSD_HEAD_DIM) + x * d_skip[:, None]
    y = y.reshape(bt, s, SSD_INNER) * jax.nn.silu(z)
    y = _rms(y, norm_g)
    h_last = states[:, -1].reshape(bt, SSD_HEADS, SSD_HEAD_DIM, SSD_STATE)
    return y, h_last, new_buf


def compress_blocks(k, w1, w2, pe):
    bt, t = k.shape[:2]
    ratio = CMP_LEN // CMP_STRIDE
    n_chunk = t // CMP_STRIDE
    n_cmp = n_chunk - ratio + 1
    ch = k[:, :n_chunk * CMP_STRIDE].reshape(bt, n_chunk, CMP_STRIDE, N_KV, HEAD_DIM)
    blocks = jnp.concatenate([ch[:, j:j + n_cmp] for j in range(ratio)], axis=2)
    blocks = blocks + pe[:, None, :]
    flat = blocks.transpose(0, 1, 3, 2, 4).reshape(bt, n_cmp, N_KV, CMP_LEN * HEAD_DIM)
    return jax.nn.silu(flat @ w1) @ w2


def nsa_compressed(q, full_cmp, q_off, w1, w2, pe):
    s = q.shape[1]
    kc = compress_blocks(full_cmp[:, :, 0], w1[0], w2[0], pe[0])
    vc = compress_blocks(full_cmp[:, :, 1], w1[1], w2[1], pe[1])
    n_cmp = kc.shape[1]
    q_pos = q_off + np.arange(s)
    ends = np.arange(n_cmp) * CMP_STRIDE + CMP_LEN - 1
    mask = ends[None, :] <= q_pos[:, None]
    sc = jnp.einsum('bsgrd,bngd->bsgrn', q, kc).astype(jnp.float32) * ATT_SCALE
    p = masked_softmax(sc, mask[None, :, None, None, :])
    return jnp.einsum('bsgrn,bngd->bsgrd', p, vc), p


def nsa_selected(q, full_sel, p_cmp, q_off):
    bt, s = q.shape[:2]
    t = full_sel.shape[1]
    n_slc = -(-t // SEL_LEN)
    n_cmp = p_cmp.shape[-1]
    ci = np.arange(n_cmp)[:, None]
    sj = np.arange(n_slc)[None, :]
    overlap = ((ci * CMP_STRIDE < (sj + 1) * SEL_LEN)
               & (ci * CMP_STRIDE + CMP_LEN > sj * SEL_LEN)).astype(np.float32)
    imp = jnp.einsum('bsgn,nj->bsgj', p_cmp.sum(3), overlap)
    q_pos = q_off + np.arange(s)
    q_blk = q_pos // SEL_LEN
    jj = np.arange(n_slc)[None, :]
    valid = jj * SEL_LEN <= q_pos[:, None]
    forced = (jj == 0) | (jj == q_blk[:, None]) | (jj == q_blk[:, None] - 1)
    score = jnp.where(valid[None, :, None, :],
                      imp + np.where(forced, FORCE_BONUS, 0.0).astype(np.float32)[None, :, None, :],
                      NEG_INF)
    kk = min(SEL_TOPK, n_slc)
    _, idx = lax.top_k(score, kk)
    kv = jnp.pad(full_sel, ((0, 0), (0, n_slc * SEL_LEN - t), (0, 0), (0, 0), (0, 0)))
    kv = kv.reshape(bt, n_slc, SEL_LEN, 2, N_KV, HEAD_DIM).transpose(0, 4, 1, 2, 3, 5)
    kt, vt = kv[..., 0, :], kv[..., 1, :]
    qb = _block(s, SEL_QBLOCK)
    nb = s // qb
    q_blocks = q.reshape(bt, nb, qb, N_KV, HEADS_PER_KV, HEAD_DIM).swapaxes(0, 1)
    idx_blocks = idx.reshape(bt, nb, qb, N_KV, kk).swapaxes(0, 1)
    pos_blocks = jnp.asarray(q_pos.reshape(nb, qb))
    bi = jnp.arange(bt)[:, None, None, None]
    gi = jnp.arange(N_KV)[None, None, :, None]
    offs = jnp.arange(SEL_LEN)

    def one_block(args):
        qblk, iblk, pblk = args
        kg = kt[bi, gi, iblk]
        vg = vt[bi, gi, iblk]
        kpos = iblk[..., None] * SEL_LEN + offs
        mask = (kpos <= pblk[None, :, None, None, None])[:, :, :, None]
        sc = jnp.einsum('bqgrd,bqgkld->bqgrkl', qblk, kg).astype(jnp.float32) * ATT_SCALE
        shp = sc.shape
        p = masked_softmax(sc.reshape(shp[:4] + (kk * SEL_LEN,)),
                           jnp.broadcast_to(mask, shp).reshape(shp[:4] + (kk * SEL_LEN,))).reshape(shp)
        return jnp.einsum('bqgrkl,bqgkld->bqgrd', p, vg)

    o = lax.map(one_block, (q_blocks, idx_blocks, pos_blocks))
    return o.swapaxes(0, 1).reshape(bt, s, N_KV, HEADS_PER_KV, HEAD_DIM)


def nsa_window(q, full_win, n_prev):
    bt, s = q.shape[:2]
    qb = _block(s, ATT_QBLOCK)
    nb = s // qb
    band = qb + WINDOW - 1
    kv = jnp.pad(full_win, ((0, 0), (WINDOW - 1, 0), (0, 0), (0, 0), (0, 0)))
    q_blocks = q.reshape(bt, nb, qb, N_KV, HEADS_PER_KV, HEAD_DIM).swapaxes(0, 1)

    def one_block(args):
        blk, qblk = args
        start = blk * qb + n_prev
        kvb = lax.dynamic_slice_in_dim(kv, start, band, axis=1)
        k_idx = start - (WINDOW - 1) + jnp.arange(band)
        q_idx = start + jnp.arange(qb)
        rel = q_idx[:, None] - k_idx[None, :]
        mask = (k_idx[None, :] >= 0) & (rel >= 0) & (rel < WINDOW)
        sc = jnp.einsum('bqgrd,blgd->bqgrl', qblk, kvb[:, :, 0]).astype(jnp.float32) * ATT_SCALE
        p = masked_softmax(sc, mask[None, :, None, None, :])
        return jnp.einsum('bqgrl,blgd->bqgrd', p, kvb[:, :, 1])

    o = lax.map(one_block, (jnp.arange(nb), q_blocks))
    return o.swapaxes(0, 1).reshape(bt, s, N_KV, HEADS_PER_KV, HEAD_DIM)


def _kv_rows(kv, branch):
    bt, s, _ = kv.shape
    return kv[..., 2 * branch * KV_W:(2 * branch + 2) * KV_W].reshape(bt, s, 2, N_KV, HEAD_DIM)


def nsa_sample_mix(q, kv, g, past_cmp, past_sel, win_buf, cmp_w1, cmp_w2, cmp_pos):
    bt, s, _ = q.shape
    q = q.reshape(bt, s, N_KV, HEADS_PER_KV, HEAD_DIM)
    new_cmp, new_sel, new_win = (_kv_rows(kv, br) for br in range(3))
    full_cmp = jnp.concatenate([past_cmp, new_cmp], 1)
    full_sel = jnp.concatenate([past_sel, new_sel], 1)
    full_win = jnp.concatenate([win_buf, new_win], 1)
    o_cmp, p_cmp = nsa_compressed(q, full_cmp, PAST_LEN, cmp_w1, cmp_w2, cmp_pos)
    o_sel = nsa_selected(q, full_sel, p_cmp, PAST_LEN)
    o_win = nsa_window(q, full_win, win_buf.shape[1])
    gate = jax.nn.sigmoid(g[..., :3 * N_HEADS]).reshape(bt, s, N_KV, HEADS_PER_KV, 3)
    o = gate[..., 0:1] * o_cmp + gate[..., 1:2] * o_sel + gate[..., 2:3] * o_win
    keep = min(WINDOW, full_win.shape[1])
    return o.reshape(bt, s, N_HEADS * HEAD_DIM), new_cmp, new_sel, full_win[:, full_win.shape[1] - keep:]


def _prompt_to_bsw(a):
    return a.reshape(SEQ, BATCH, -1).transpose(1, 0, 2)


def _prompt_from_bsw(a):
    return a.transpose(1, 0, 2).reshape(T_PROMPT, -1)


def _sample_to_bsw(a):
    return a.reshape(SAMPLE_GROUPS, DEC_SEQ, SUBLANE, -1).transpose(0, 2, 1, 3).reshape(DEC_BATCH, DEC_SEQ, -1)


def _sample_from_bsw(a):
    return a.reshape(SAMPLE_GROUPS, SUBLANE, DEC_SEQ, -1).transpose(0, 2, 1, 3).reshape(T_SAMPLE, -1)


def _pad_cols(w, n):
    return jnp.pad(w, ((0, 0), (0, n - w.shape[1])))


def _rope_tables():
    half = ROPE_DIM // 2
    pos = jnp.concatenate([jnp.repeat(jnp.arange(SEQ), BATCH),
                           jnp.tile(jnp.repeat(PAST_LEN + jnp.arange(DEC_SEQ), SUBLANE), SAMPLE_GROUPS)])
    inv = ROPE_THETA ** (-jnp.arange(half, dtype=F32) / half)
    ang = pos.astype(F32)[:, None] * inv
    cos, sin = jnp.cos(ang), jnp.sin(ang)
    ones = jnp.ones((T_ALL, HEAD_DIM - ROPE_DIM), F32)
    zeros = jnp.zeros((T_ALL, HEAD_DIM - ROPE_DIM), F32)
    zh = jnp.zeros((T_ALL, half), F32)
    two = lambda parts: jnp.tile(jnp.concatenate(parts, -1), (1, LANE // HEAD_DIM))
    return two([cos, cos, ones]), two([zh, sin, zeros]), two([-sin, zh, zeros])


def kernel(x_prompt, x_sample, state_s5, state_ssd, state_conv, cache_cmp_kv, cache_sel_kv, state_win_kv, page_table, norm_mix_even, w_in_even, s5_a_re, s5_a_im, s5_log_dt, s5_b_re, s5_b_im, s5_c_re, s5_c_im, s5_d, s5_glu_w, s5_glu_b, ssd_conv_w, ssd_conv_b, ssd_dt_bias, ssd_a_log, ssd_d, ssd_norm, w_out_even, norm_mix_odd, w_in_odd, cmp_w1, cmp_w2, cmp_pos, w_out_odd, norm_mlp, w_up, w_down, norm_final):
    h = jnp.concatenate([_prompt_from_bsw(x_prompt), _sample_from_bsw(x_sample)], 0)
    even_widths = (S5_WIDTH, SSD_INNER, SSD_CONV_DIM, _pad_lanes(SSD_HEADS))
    odd_widths = (N_HEADS * HEAD_DIM, 6 * KV_W, _pad_lanes(3 * N_HEADS))
    rope_tabs = _rope_tables()
    rope_blocks = (tuple(range(N_HEADS * HEAD_DIM // LANE)), (0, 2, 4), ())
    outs = {k: [] for k in ("s5_p", "s5_s", "ssd_p", "ssd_s", "conv_p", "conv_s",
                            "cmp_p", "cmp_s", "sel_p", "sel_s", "win_p", "win_s")}
    for layer in range(DEPTH):
        i = layer // 2
        if layer % 2 == 0:
            w_in = _pad_cols(w_in_even[i], sum(even_widths)).astype(BF16)
            u, z, xbc, dt_raw = norm_proj(h, norm_mix_even[i], w_in, even_widths)
            tabs = s5_tables(s5_a_re[i], s5_a_im[i], s5_log_dt[i], s5_b_re[i], s5_b_im[i], s5_c_re[i], s5_c_im[i],
                             s5_d[i], s5_glu_w[i], s5_glu_b[i])
            ya_p, s5_hp = s5_scan(u[:T_PROMPT], jnp.zeros((BATCH, 2 * S5_GROUPS * S5_STATE), F32), tabs,
                                  n_seq=1, t_len=S5_PROMPT_STEPS)
            ya_s, s5_hs = s5_scan(u[T_PROMPT:], _s5_state_to_lanes(state_s5[i]), tabs,
                                  n_seq=SAMPLE_GROUPS, t_len=DEC_SEQ)
            outs["s5_p"].append(_s5_state_from_lanes(s5_hp))
            outs["s5_s"].append(_s5_state_from_lanes(s5_hs))
            ssdw = (ssd_conv_w[i], ssd_conv_b[i], ssd_dt_bias[i], ssd_a_log[i], ssd_d[i], ssd_norm[i])
            tm = lambda a: a[:T_PROMPT].reshape(SEQ, -1)
            yb_p, ssd_h, conv_new = ssd_prompt(tm(z), tm(xbc), tm(dt_raw), *ssdw)
            outs["ssd_p"].append(ssd_h.reshape(BATCH, SSD_HEADS, SSD_HEAD_DIM, SSD_STATE))
            outs["conv_p"].append(conv_new)
            yb_s, ssd_h, conv_new = ssd_mixer(
                _sample_to_bsw(z[T_PROMPT:]), _sample_to_bsw(xbc[T_PROMPT:]),
                _sample_to_bsw(dt_raw[T_PROMPT:, :SSD_HEADS]), state_conv[i], state_ssd[i], *ssdw)
            outs["ssd_s"].append(ssd_h)
            outs["conv_s"].append(conv_new)
            yb = jnp.concatenate([yb_p.reshape(T_PROMPT, -1), _sample_from_bsw(yb_s)], 0)
            h = out_proj(h, [jnp.concatenate([ya_p, ya_s], 0), yb], w_out_even[i].astype(BF16))
        else:
            w_in = _pad_cols(w_in_odd[i], sum(odd_widths)).astype(BF16)
            q, kv, g = norm_proj(h, norm_mix_odd[i], w_in, odd_widths, rope_tabs, rope_blocks)
            kv_p = _prompt_to_bsw(kv[:T_PROMPT])
            new_cmp, new_sel, new_win = (_kv_rows(kv_p, br) for br in range(3))
            kc = compress_blocks(new_cmp[:, :, 0], cmp_w1[i][0], cmp_w2[i][0], cmp_pos[i][0])
            vc = compress_blocks(new_cmp[:, :, 1], cmp_w1[i][1], cmp_w2[i][1], cmp_pos[i][1])
            pad_c = lambda a: jnp.pad(a.reshape(BATCH, N_CMP_PROMPT, KV_W), ((0, 0), (0, LANE - N_CMP_PROMPT), (0, 0)))
            y_p = nsa_prompt(q[:T_PROMPT].reshape(SEQ, -1), g[:T_PROMPT].reshape(SEQ, -1),
                             kv[:T_PROMPT].reshape(SEQ, -1), pad_c(kc), pad_c(vc)).reshape(T_PROMPT, -1)
            outs["cmp_p"].append(new_cmp)
            outs["sel_p"].append(new_sel)
            outs["win_p"].append(new_win[:, SEQ - WINDOW:])
            kv_s = _sample_to_bsw(kv[T_PROMPT:])
            y_s = nsa_sample_layer(i, _sample_to_bsw(q[T_PROMPT:]), kv_s, _sample_to_bsw(g[T_PROMPT:]), page_table,
                                   cache_cmp_kv, cache_sel_kv, state_win_kv, cmp_w1[i], cmp_w2[i], cmp_pos[i])
            new_cmp, new_sel, new_win = (_kv_rows(kv_s, br) for br in range(3))
            outs["cmp_s"].append(new_cmp)
            outs["sel_s"].append(new_sel)
            outs["win_s"].append(jnp.concatenate([state_win_kv[i][:, DEC_SEQ:], new_win], 1))
            y = jnp.concatenate([y_p, _sample_from_bsw(y_s)], 0)
            h = out_proj(h, [y], w_out_odd[i].astype(BF16))
        h = mlp(h, norm_mlp[layer], w_up[layer].astype(BF16), w_down[layer].astype(BF16),
                norm_final, final=(layer == DEPTH - 1))
    y_prompt, y_sample = _prompt_to_bsw(h[:T_PROMPT]), _sample_to_bsw(h[T_PROMPT:])
    st = {k: jnp.stack(v) for k, v in outs.items()}
    return (y_prompt, y_sample, st["s5_p"], st["s5_s"], st["ssd_p"], st["ssd_s"], st["conv_p"], st["conv_s"],
            st["cmp_p"], st["cmp_s"], st["sel_p"], st["sel_s"], st["win_p"], st["win_s"])
```

```python
import functools

import jax
import jax.numpy as jnp
import numpy as np
from jax import lax
from jax.experimental import pallas as pl
from jax.experimental.pallas import tpu as pltpu

D_MODEL = 1024
BATCH = 8
SEQ = 2048
DEPTH = 4
DEC_BATCH = 128
DEC_SEQ = 4
PAST_LEN = 2048
PAGE_SIZE = 128

N_SSM = (DEPTH + 1) // 2
N_ATT = DEPTH // 2
NORM_EPS = 1e-5
D_FF = 4 * D_MODEL
NEG_INF = -1e30

S5_WIDTH = D_MODEL // 2
S5_GROUP = 16
S5_GROUPS = S5_WIDTH // S5_GROUP
S5_STATE = 64

SSD_INNER = D_MODEL
SSD_HEAD_DIM = 64
SSD_HEADS = SSD_INNER // SSD_HEAD_DIM
SSD_STATE = 128
SSD_GROUPS = 4
SSD_CONV = 4
SSD_CONV_DIM = SSD_INNER + 2 * SSD_GROUPS * SSD_STATE
SSD_CHUNK = 128
MIX_EVEN = S5_WIDTH + SSD_INNER
IN_EVEN = S5_WIDTH + SSD_INNER + SSD_CONV_DIM + SSD_HEADS

N_HEADS = 16
HEAD_DIM = D_MODEL // N_HEADS
N_KV = 2
HEADS_PER_KV = N_HEADS // N_KV
KV_W = N_KV * HEAD_DIM
ROPE_DIM = HEAD_DIM // 4
ROPE_THETA = 500000.0
ATT_SCALE = HEAD_DIM ** -0.5
CMP_LEN = 32
CMP_STRIDE = 16
CMP_HIDDEN = 2 * HEAD_DIM
SEL_LEN = 64
SEL_TOPK = 16
FORCE_BONUS = 1e4
WINDOW = 512
ATT_QBLOCK = 128
SEL_QBLOCK = 64
IN_ODD = N_HEADS * HEAD_DIM + 6 * KV_W + 3 * N_HEADS

T_PROMPT = BATCH * SEQ
T_SAMPLE = DEC_BATCH * DEC_SEQ
T_ALL = T_PROMPT + T_SAMPLE

LANE = 128
SUBLANE = 8
TOKEN_TILE = 512
FF_CHUNK = 1024
VMEM_LIMIT = 56 * 1024 * 1024

S5_SLAB_GROUPS = LANE // S5_GROUP
S5_SLABS = S5_GROUPS // S5_SLAB_GROUPS
S5_SLAB_STATE = S5_SLAB_GROUPS * S5_STATE
S5_PROMPT_STEPS = 256
SAMPLE_GROUPS = DEC_BATCH // SUBLANE

NSA_Q_TILE = 256
NSA_K_TILE = 256
N_SLC_PROMPT = SEQ // SEL_LEN
N_CMP_PROMPT = SEQ // CMP_STRIDE - CMP_LEN // CMP_STRIDE + 1

F32 = jnp.float32
BF16 = jnp.bfloat16


def _pad_lanes(n):
    return -(-n // LANE) * LANE


def _rms(x, g):
    return x * lax.rsqrt(jnp.mean(x * x, -1, keepdims=True) + NORM_EPS) * g


def _resident(shape):
    return pl.BlockSpec(shape, lambda *_: (0,) * len(shape), pipeline_mode=pl.Buffered(1))


def _rows(width):
    return pl.BlockSpec((TOKEN_TILE, width), lambda i: (i, 0))


def _dot(a, b):
    return jnp.dot(a, b, preferred_element_type=F32)


def _dot_nt(a, b):
    return lax.dot_general(a, b, (((1,), (1,)), ((), ())), preferred_element_type=F32)


def _split_bf16(x):
    hi = x.astype(BF16)
    return hi, (x - hi.astype(F32)).astype(BF16)


_PARAMS = pltpu.CompilerParams(dimension_semantics=("arbitrary",), vmem_limit_bytes=VMEM_LIMIT)
_PARAMS2 = pltpu.CompilerParams(dimension_semantics=("arbitrary", "arbitrary"), vmem_limit_bytes=VMEM_LIMIT)


def _rope_block(x, cos_f, sin_a, sin_b):
    return x * cos_f + pltpu.roll(x, ROPE_DIM // 2, 1) * sin_a + pltpu.roll(x, LANE - ROPE_DIM // 2, 1) * sin_b


def _norm_proj_body(x_ref, g_ref, w_ref, *refs, widths, rope_blocks):
    if rope_blocks is None:
        o_refs = refs
    else:
        cos_ref, sa_ref, sb_ref = refs[:3]
        o_refs = refs[3:]
    xn = _rms(x_ref[...], g_ref[...]).astype(BF16)
    off = 0
    for idx, (o_ref, wd) in enumerate(zip(o_refs, widths)):
        o_ref[...] = _dot(xn, w_ref[:, off:off + wd])
        off += wd
        if rope_blocks is not None:
            for blk in rope_blocks[idx]:
                cols = slice(blk * LANE, (blk + 1) * LANE)
                o_ref[:, cols] = _rope_block(o_ref[:, cols], cos_ref[...], sa_ref[...], sb_ref[...])


def norm_proj(x, g, w, widths, rope_tabs=None, rope_blocks=None):
    n = sum(widths)
    in_specs = [_rows(D_MODEL), _resident((1, D_MODEL)), _resident((D_MODEL, n))]
    args = [x, g.reshape(1, D_MODEL), w]
    if rope_blocks is not None:
        in_specs += [_rows(LANE)] * 3
        args += list(rope_tabs)
    return pl.pallas_call(
        functools.partial(_norm_proj_body, widths=widths, rope_blocks=rope_blocks),
        grid=(T_ALL // TOKEN_TILE,),
        in_specs=in_specs,
        out_specs=[_rows(wd) for wd in widths],
        out_shape=[jax.ShapeDtypeStruct((T_ALL, wd), F32) for wd in widths],
        compiler_params=_PARAMS,
        name="norm_proj",
    )(*args)


def _out_proj_body(x_ref, *refs, widths):
    y_refs, w_ref, o_ref = refs[:len(widths)], refs[len(widths)], refs[len(widths) + 1]
    acc = x_ref[...]
    off = 0
    for y_ref, wd in zip(y_refs, widths):
        acc = acc + _dot(y_ref[...].astype(BF16), w_ref[off:off + wd, :])
        off += wd
    o_ref[...] = acc


def out_proj(x, ys, w):
    widths = tuple(y.shape[1] for y in ys)
    return pl.pallas_call(
        functools.partial(_out_proj_body, widths=widths),
        grid=(T_ALL // TOKEN_TILE,),
        in_specs=[_rows(D_MODEL)] + [_rows(wd) for wd in widths] + [_resident((sum(widths), D_MODEL))],
        out_specs=_rows(D_MODEL),
        out_shape=jax.ShapeDtypeStruct((T_ALL, D_MODEL), F32),
        compiler_params=_PARAMS,
        name="out_proj",
    )(x, *ys, w)


def _mlp_body(x_ref, g_ref, wu_ref, wd_ref, gf_ref, o_ref, *, final):
    x = x_ref[...]
    xn = _rms(x, g_ref[...]).astype(BF16)
    acc = x
    for c in range(D_FF // FF_CHUNK):
        cols = slice(c * FF_CHUNK, (c + 1) * FF_CHUNK)
        h = jnp.maximum(_dot(xn, wu_ref[:, cols]), 0.0)
        acc = acc + _dot((h * h).astype(BF16), wd_ref[cols, :])
    if final:
        acc = _rms(acc, gf_ref[...])
    o_ref[...] = acc


def mlp(x, g, w_up, w_down, g_final, final):
    return pl.pallas_call(
        functools.partial(_mlp_body, final=final),
        grid=(T_ALL // TOKEN_TILE,),
        in_specs=[_rows(D_MODEL), _resident((1, D_MODEL)), _resident((D_MODEL, D_FF)),
                  _resident((D_FF, D_MODEL)), _resident((1, D_MODEL))],
        out_specs=_rows(D_MODEL),
        out_shape=jax.ShapeDtypeStruct((T_ALL, D_MODEL), F32),
        compiler_params=_PARAMS,
        name="mlp",
    )(x, g.reshape(1, D_MODEL), w_up, w_down, g_final.reshape(1, D_MODEL))


def _cmul(ar, ai, br, bi):
    return ar * br - ai * bi, ar * bi + ai * br


def s5_tables(a_re, a_im, log_dt, b_re, b_im, c_re, c_im, d_skip, glu_w, glu_b):
    dt = jnp.exp(log_dt)[:, None]
    mag = jnp.exp(a_re * dt)
    abar_re, abar_im = mag * jnp.cos(a_im * dt), mag * jnp.sin(a_im * dt)
    den = a_re * a_re + a_im * a_im
    f_re = ((abar_re - 1.0) * a_re + abar_im * a_im) / den
    f_im = (abar_im * a_re - (abar_re - 1.0) * a_im) / den
    bbar_re, bbar_im = _cmul(f_re[..., None], f_im[..., None], b_re, b_im)
    sg, ns = S5_SLAB_GROUPS, S5_SLABS
    eye = jnp.eye(sg, dtype=F32)
    a_tab = jnp.stack([abar_re.reshape(ns, S5_SLAB_STATE), abar_im.reshape(ns, S5_SLAB_STATE)], 1)

    def in_blockdiag(bb):
        return jnp.einsum('jgpc,gh->jgchp', bb.reshape(ns, sg, S5_STATE, S5_GROUP), eye).reshape(ns, LANE, S5_SLAB_STATE)

    def out_blockdiag(cc):
        return jnp.einsum('jgcp,gh->jhpgc', cc.reshape(ns, sg, S5_GROUP, S5_STATE), eye).reshape(ns, S5_SLAB_STATE, LANE)

    b_bd = jnp.concatenate([in_blockdiag(bbar_re), in_blockdiag(bbar_im)], -1).astype(BF16)
    c_bd = jnp.concatenate([out_blockdiag(c_re), out_blockdiag(-c_im)], 1).astype(BF16)
    g_bd = jnp.einsum('jgcke,gh->jgckhe', glu_w.reshape(ns, sg, S5_GROUP, 2, S5_GROUP), eye)
    g_bd = g_bd.reshape(ns, LANE, 2 * LANE).astype(BF16)
    g_b = glu_b.reshape(ns, sg, 2, S5_GROUP).transpose(0, 2, 1, 3).reshape(ns, 1, 2 * LANE)
    return a_tab, b_bd, c_bd, d_skip.reshape(ns, 1, LANE), g_bd, g_b


def _s5_body(u_ref, h0_ref, a_ref, b_ref, c_ref, d_ref, gw_ref, gb_ref, o_ref, hl_ref, st_ref, h_ref, *,
             n_seq, t_len):
    w = S5_SLAB_STATE

    @pl.when(pl.program_id(1) == 0)
    def _():
        h_ref[...] = h0_ref[...]

    ub = u_ref[...]
    st_ref[...] = _dot(ub.astype(BF16), b_ref[0])
    ar = jnp.broadcast_to(a_ref[0, 0:1, :], (SUBLANE, w))
    ai = jnp.broadcast_to(a_ref[0, 1:2, :], (SUBLANE, w))
    for s in range(n_seq):
        base = s * t_len * SUBLANE
        rows = slice(s * SUBLANE, (s + 1) * SUBLANE)

        def step(t, carry, base=base):
            hr, hi = carry
            r = pl.ds(pl.multiple_of(base + t * SUBLANE, SUBLANE), SUBLANE)
            nhr = ar * hr - ai * hi + st_ref[r, 0:w]
            nhi = ar * hi + ai * hr + st_ref[r, w:2 * w]
            st_ref[r, 0:w] = nhr
            st_ref[r, w:2 * w] = nhi
            return nhr, nhi

        hr, hi = lax.fori_loop(0, t_len, step, (h_ref[rows, 0:w], h_ref[rows, w:2 * w]), unroll=min(t_len, 8))
        h_ref[rows, 0:w] = hr
        h_ref[rows, w:2 * w] = hi
    y = _dot(st_ref[...].astype(BF16), c_ref[0]) + d_ref[0] * ub
    zg = _dot(y.astype(BF16), gw_ref[0]) + gb_ref[0]
    o_ref[...] = zg[:, :LANE] * jax.nn.sigmoid(zg[:, LANE:])
    hl_ref[...] = h_ref[...]


def s5_scan(u, h0, tabs, *, n_seq, t_len):
    blk = n_seq * t_len * SUBLANE
    n_chunks = u.shape[0] // blk
    assert n_chunks * blk == u.shape[0] and (n_seq == 1 or n_chunks == 1)
    hb = n_seq * SUBLANE
    slab = lambda shape: pl.BlockSpec((1,) + shape, lambda j, c: (j, 0, 0))
    return pl.pallas_call(
        functools.partial(_s5_body, n_seq=n_seq, t_len=t_len),
        grid=(S5_SLABS, n_chunks),
        in_specs=[pl.BlockSpec((blk, LANE), lambda j, c: (c, j)),
                  pl.BlockSpec((hb, 2 * S5_SLAB_STATE), lambda j, c: (0, j)),
                  slab((2, S5_SLAB_STATE)), slab((LANE, 2 * S5_SLAB_STATE)), slab((2 * S5_SLAB_STATE, LANE)),
                  slab((1, LANE)), slab((LANE, 2 * LANE)), slab((1, 2 * LANE))],
        out_specs=[pl.BlockSpec((blk, LANE), lambda j, c: (c, j)),
                   pl.BlockSpec((hb, 2 * S5_SLAB_STATE), lambda j, c: (0, j))],
        out_shape=[jax.ShapeDtypeStruct(u.shape, F32), jax.ShapeDtypeStruct(h0.shape, F32)],
        scratch_shapes=[pltpu.VMEM((blk, 2 * S5_SLAB_STATE), F32), pltpu.VMEM((hb, 2 * S5_SLAB_STATE), F32)],
        compiler_params=_PARAMS2,
        name="s5_scan",
    )(u, h0, *tabs)


def _s5_state_to_lanes(h):
    bt = h.shape[0]
    return h.reshape(bt, S5_SLABS, S5_SLAB_STATE, 2).transpose(0, 1, 3, 2).reshape(bt, -1)


def _s5_state_from_lanes(h):
    bt = h.shape[0]
    return h.reshape(bt, S5_SLABS, 2, S5_SLAB_STATE).transpose(0, 1, 3, 2).reshape(bt, S5_GROUPS, S5_STATE, 2)


SSD_BC = SSD_GROUPS * SSD_STATE
SSD_GROUP_W = SSD_INNER // SSD_GROUPS
CONV_PAD = SUBLANE


def _ssd_consts():
    tri = np.tril(np.ones((SSD_CHUNK, SSD_CHUNK), np.float32))
    expand = np.zeros((LANE, SSD_INNER), np.float32)
    for h in range(SSD_HEADS):
        expand[h, h * SSD_HEAD_DIM:(h + 1) * SSD_HEAD_DIM] = 1.0
    return jnp.asarray(tri, BF16), jnp.asarray(expand, BF16)


def _split3_dot(m, x):
    hi, r = _split_bf16(x)
    x2 = x - hi.astype(F32) - r.astype(F32)
    return _dot(m, hi) + _dot(m, r) + _dot(m, x2.astype(BF16))


def _expand_heads(x, eh):
    hi, lo = _split_bf16(x)
    return _dot(hi, eh) + _dot(lo, eh)


def _ssd_prompt_body(z_ref, x_ref, dt_ref, cw_ref, cb_ref, dtb_ref, alog_ref, d_ref, ng_ref, tri_ref, eh_ref,
                     y_ref, hl_ref, cl_ref, xs_ref, h_ref):
    n = SSD_CHUNK
    k0 = CONV_PAD - (SSD_CONV - 1)

    @pl.when(pl.program_id(1) == 0)
    def _():
        h_ref[...] = jnp.zeros(h_ref.shape, F32)
        xs_ref[0:CONV_PAD, :] = jnp.zeros((CONV_PAD, SSD_CONV_DIM), F32)

    xs_ref[CONV_PAD:CONV_PAD + n, :] = x_ref[...]
    conv = cb_ref[...]
    for k in range(SSD_CONV):
        conv = conv + xs_ref[k0 + k:k0 + k + n, :] * cw_ref[k:k + 1, :]
    tail = xs_ref[k0 + n:CONV_PAD + n, :]
    xs_ref[k0:CONV_PAD, :] = tail
    cl_ref[0] = tail
    xa = conv * jax.nn.sigmoid(conv)
    x = xa[:, :SSD_INNER]

    dt = jax.nn.softplus(dt_ref[...] + dtb_ref[...])
    a_dt = dt * (-jnp.exp(alog_ref[...]))
    a_cs = _split3_dot(tri_ref[...], a_dt)
    a_cs_t = a_cs.T
    dt_t = dt.T
    a_tot = a_cs[n - 1:n, :]
    eh = eh_ref[...]
    decay_in = _expand_heads(jnp.exp(a_cs), eh)
    xw = x * _expand_heads(jnp.exp(a_tot - a_cs) * dt, eh)
    lower = (lax.broadcasted_iota(jnp.int32, (n, n), 0) >= lax.broadcasted_iota(jnp.int32, (n, n), 1))
    low = lax.broadcasted_iota(jnp.int32, (n, LANE), 1) < SSD_HEAD_DIM
    heads_per_group = SSD_HEADS // SSD_GROUPS
    for g in range(SSD_GROUPS):
        b_g = xa[:, SSD_INNER + g * SSD_STATE:SSD_INNER + (g + 1) * SSD_STATE].astype(BF16)
        c_g = xa[:, SSD_INNER + SSD_BC + g * SSD_STATE:SSD_INNER + SSD_BC + (g + 1) * SSD_STATE].astype(BF16)
        gcols = slice(g * SSD_GROUP_W, (g + 1) * SSD_GROUP_W)
        h_g = h_ref[gcols, :]
        cb = _dot_nt(c_g, b_g)
        y_off = _dot_nt(c_g, h_g.astype(BF16)) * decay_in[:, gcols]
        for pr in range(heads_per_group // 2):
            cols = slice(g * SSD_GROUP_W + pr * LANE, g * SSD_GROUP_W + (pr + 1) * LANE)
            x_pair = x[:, cols]
            y_pair = y_off[:, pr * LANE:(pr + 1) * LANE] + x_pair * d_ref[:, cols]
            for side in range(2):
                h = g * heads_per_group + 2 * pr + side
                seg = jnp.where(lower, jnp.exp(a_cs[:, h:h + 1] - a_cs_t[h:h + 1, :]), 0.0)
                m = (cb * seg * dt_t[h:h + 1, :]).astype(BF16)
                x_side = jnp.where(low if side == 0 else ~low, x_pair, 0.0).astype(BF16)
                y_pair = y_pair + _dot(m, x_side)
            y_ref[:, cols] = y_pair
        upd = _dot(xw[:, gcols].T.astype(BF16), b_g)
        for hh in range(heads_per_group):
            h = g * heads_per_group + hh
            rows = slice(g * SSD_GROUP_W + hh * SSD_HEAD_DIM, g * SSD_GROUP_W + (hh + 1) * SSD_HEAD_DIM)
            scale = jnp.exp(jnp.broadcast_to(a_tot[:, h:h + 1], (SSD_HEAD_DIM, SSD_STATE)))
            h_ref[rows, :] = h_ref[rows, :] * scale + upd[hh * SSD_HEAD_DIM:(hh + 1) * SSD_HEAD_DIM, :]
    z = z_ref[...]
    y_ref[...] = _rms(y_ref[...] * (z * jax.nn.sigmoid(z)), ng_ref[...])
    hl_ref[0] = h_ref[...]


def ssd_prompt(z, xbc, dt_raw, conv_w, conv_b, dt_bias, a_log, d_skip, norm_g):
    n = SSD_CHUNK
    nb = z.shape[1] // SSD_INNER
    tri, eh = _ssd_consts()
    lanes = lambda v: jnp.pad(v, (0, LANE - v.shape[0])).reshape(1, LANE)
    chunk = lambda w: pl.BlockSpec((n, w), lambda b, c: (c, b))
    return pl.pallas_call(
        _ssd_prompt_body,
        grid=(nb, SEQ // n),
        in_specs=[chunk(SSD_INNER), chunk(SSD_CONV_DIM), chunk(LANE),
                  _resident((SSD_CONV, SSD_CONV_DIM)), _resident((1, SSD_CONV_DIM)), _resident((1, LANE)),
                  _resident((1, LANE)), _resident((1, SSD_INNER)), _resident((1, SSD_INNER)),
                  _resident(tri.shape), _resident(eh.shape)],
        out_specs=[chunk(SSD_INNER),
                   pl.BlockSpec((1, SSD_INNER, SSD_STATE), lambda b, c: (b, 0, 0)),
                   pl.BlockSpec((1, SSD_CONV - 1, SSD_CONV_DIM), lambda b, c: (b, 0, 0))],
        out_shape=[jax.ShapeDtypeStruct((SEQ, nb * SSD_INNER), F32),
                   jax.ShapeDtypeStruct((nb, SSD_INNER, SSD_STATE), F32),
                   jax.ShapeDtypeStruct((nb, SSD_CONV - 1, SSD_CONV_DIM), F32)],
        scratch_shapes=[pltpu.VMEM((CONV_PAD + n, SSD_CONV_DIM), F32), pltpu.VMEM((SSD_INNER, SSD_STATE), F32)],
        compiler_params=_PARAMS2,
        name="ssd_prompt",
    )(z, xbc, dt_raw, conv_w, conv_b.reshape(1, -1), lanes(dt_bias), lanes(a_log),
      jnp.repeat(d_skip, SSD_HEAD_DIM).reshape(1, -1), norm_g.reshape(1, -1), tri, eh)


def _nsa_consts():
    nq = N_SLC_PROMPT
    ci = np.arange(LANE)[None, :]
    sj = np.arange(nq)[:, None]
    overlap_t = ((ci * CMP_STRIDE < (sj + 1) * SEL_LEN) & (ci * CMP_STRIDE + CMP_LEN > sj * SEL_LEN)
                 & (ci < N_CMP_PROMPT)).astype(np.float32)
    keys = np.arange(SEQ)
    expand = (keys[:, None] // SEL_LEN == np.arange(nq)[None, :]).astype(np.float32)
    expand = expand.reshape(SEQ // NSA_K_TILE, NSA_K_TILE, nq)
    gate = np.zeros((3, N_HEADS * HEAD_DIM, LANE), np.float32)
    for br in range(3):
        for h in range(N_HEADS):
            gate[br, h * HEAD_DIM:(h + 1) * HEAD_DIM, h * 3 + br] = 1.0
    return jnp.asarray(overlap_t, BF16), jnp.asarray(expand, BF16), jnp.asarray(gate, BF16)


def _k_variants(k):
    low = lax.broadcasted_iota(jnp.int32, k.shape, 1) < HEAD_DIM
    k = k * ATT_SCALE
    k_sw = pltpu.roll(k, HEAD_DIM, 1)
    kl = (jnp.where(low, k, 0.0).astype(BF16), jnp.where(low, k_sw, 0.0).astype(BF16))
    kr = (jnp.where(low, 0.0, k_sw).astype(BF16), jnp.where(low, 0.0, k).astype(BF16))
    return kl, kr


def _vt_variants(v):
    vt = v.T
    zero = jnp.zeros((HEAD_DIM, v.shape[0]), F32)
    top, bot = vt[:HEAD_DIM], vt[HEAD_DIM:]
    vl = (jnp.concatenate([top, zero], 0).astype(BF16), jnp.concatenate([bot, zero], 0).astype(BF16))
    vr = (jnp.concatenate([zero, top], 0).astype(BF16), jnp.concatenate([zero, bot], 0).astype(BF16))
    return vl, vr


def _nsa_prompt_body(q_ref, g_ref, kv_ref, kc_ref, vc_ref, ov_ref, ex_ref, eg_ref, o_ref,
                     kl_ref, kr_ref, vl_ref, vr_ref, acc_ref, m_ref, l_ref, ob_ref):
    tq = NSA_Q_TILE
    tk = NSA_K_TILE
    n_pairs = HEADS_PER_KV // 2
    qi = pl.program_id(1)
    q0 = qi * tq

    @pl.when(qi == 0)
    def _():
        for br in range(2):
            c0 = (2 + 2 * br) * LANE
            kl, kr = _k_variants(kv_ref[:, c0:c0 + LANE])
            for g in range(N_KV):
                kl_ref[br, g] = kl[g]
                kr_ref[br, g] = kr[g]
            for t in range(SEQ // tk):
                vl, vr = _vt_variants(kv_ref[t * tk:(t + 1) * tk, c0 + LANE:c0 + 2 * LANE])
                for g in range(N_KV):
                    vl_ref[br, g, t] = vl[g]
                    vr_ref[br, g, t] = vr[g]

    row_low = lax.broadcasted_iota(jnp.int32, (LANE, tq), 0) < HEAD_DIM
    kpos_t = lax.broadcasted_iota(jnp.int32, (tk, tq), 0)
    qpos_t = q0 + lax.broadcasted_iota(jnp.int32, (tk, tq), 1)
    kcl, kcr = _k_variants(kc_ref[0])
    vcl, vcr = _vt_variants(vc_ref[0])
    n_iota = lax.broadcasted_iota(jnp.int32, (LANE, tq), 0)
    cmp_mask = ((n_iota * CMP_STRIDE + (CMP_LEN - 1) <= q0 + lax.broadcasted_iota(jnp.int32, (LANE, tq), 1))
                & (n_iota < N_CMP_PROMPT))

    def pair_q(g, pr):
        c0 = (g * n_pairs + pr) * LANE
        return q_ref[:, c0:c0 + LANE].astype(BF16)

    def flash(br, g, n_tiles, tile0, mask_fn, reverse):
        m_ref[...] = jnp.full(m_ref.shape, NEG_INF, F32)
        l_ref[...] = jnp.zeros(l_ref.shape, F32)
        acc_ref[...] = jnp.zeros(acc_ref.shape, F32)

        def tile(kt, carry):
            if reverse:
                kt = n_tiles - 1 - kt
            kt = tile0 + kt
            k0 = pl.multiple_of(kt * tk, tk)
            mask = mask_fn(kt, k0)
            for pr in range(n_pairs):
                qp = pair_q(g, pr)
                ps = []
                alpha = []
                for side, k_ref in enumerate((kl_ref, kr_ref)):
                    hh = 2 * pr + side
                    s = jnp.where(mask, _dot_nt(k_ref[br, g, pl.ds(k0, tk), :], qp), NEG_INF)
                    m_old = m_ref[hh:hh + 1, :]
                    m_new = jnp.maximum(m_old, jnp.max(s, 0, keepdims=True))
                    p = jnp.exp(s - m_new)
                    a = jnp.exp(m_old - m_new)
                    l_ref[hh:hh + 1, :] = a * l_ref[hh:hh + 1, :] + jnp.sum(p, 0, keepdims=True)
                    m_ref[hh:hh + 1, :] = m_new
                    ps.append(p.astype(BF16))
                    alpha.append(a)
                pv = _dot(vl_ref[br, g, kt], ps[0]) + _dot(vr_ref[br, g, kt], ps[1])
                acc_ref[pr] = acc_ref[pr] * jnp.where(row_low, alpha[0], alpha[1]) + pv
            return carry

        lax.fori_loop(0, n_tiles, tile, 0)
        for pr in range(n_pairs):
            r0 = (g * n_pairs + pr) * LANE
            l_pair = jnp.where(row_low, l_ref[2 * pr:2 * pr + 1, :], l_ref[2 * pr + 1:2 * pr + 2, :])
            ob_ref[br + 1, r0:r0 + LANE, :] = acc_ref[pr] / l_pair

    for g in range(N_KV):
        psum = jnp.zeros((LANE, tq), F32)
        for pr in range(n_pairs):
            qp = pair_q(g, pr)
            ps = []
            for kc in (kcl[g], kcr[g]):
                s = jnp.where(cmp_mask, _dot_nt(kc, qp), NEG_INF)
                p = jnp.where(cmp_mask, jnp.exp(s - jnp.max(s, 0, keepdims=True)), 0.0)
                p = p / jnp.maximum(jnp.sum(p, 0, keepdims=True), 1e-30)
                psum = psum + p
                ps.append(p.astype(BF16))
            r0 = (g * n_pairs + pr) * LANE
            ob_ref[0, r0:r0 + LANE, :] = _dot(vcl[g], ps[0]) + _dot(vcr[g], ps[1])

        p_hi, p_lo = _split_bf16(psum)
        imp = _dot(ov_ref[...], p_hi) + _dot(ov_ref[...], p_lo)
        shp = (N_SLC_PROMPT, tq)
        jj = lax.broadcasted_iota(jnp.int32, shp, 0)
        qp_t = q0 + lax.broadcasted_iota(jnp.int32, shp, 1)
        qb_t = qp_t // SEL_LEN
        forced = (jj == 0) | (jj == qb_t) | (jj == qb_t - 1)
        score = jnp.where(jj * SEL_LEN <= qp_t, imp + jnp.where(forced, FORCE_BONUS, 0.0), NEG_INF)
        rank = jnp.zeros(shp, F32)
        for i in range(N_SLC_PROMPT):
            row = score[i:i + 1, :]
            ahead = (row > score) | ((row == score) & (jj > i))
            rank = rank + jnp.where(ahead, 1.0, 0.0)
        sel_t = jnp.where(rank < SEL_TOPK, 1.0, 0.0).astype(BF16)

        def sel_mask(kt, k0, sel_t=sel_t):
            chosen = _dot(ex_ref[kt], sel_t) > 0.5
            return chosen & (k0 + kpos_t <= qpos_t)

        flash(0, g, (q0 + tq) // tk, 0, sel_mask, reverse=False)

        def win_mask(kt, k0):
            rel = qpos_t - (k0 + kpos_t)
            return (rel >= 0) & (rel < WINDOW)

        w0 = jnp.maximum(q0 - WINDOW, 0) // tk
        flash(1, g, (q0 + tq) // tk - w0, w0, win_mask, reverse=True)

    g_hi, g_lo = _split_bf16(jax.nn.sigmoid(g_ref[...]))
    out_t = jnp.zeros((N_HEADS * HEAD_DIM, tq), F32)
    for br in range(3):
        out_t = out_t + (_dot_nt(eg_ref[br], g_hi) + _dot_nt(eg_ref[br], g_lo)) * ob_ref[br]
    for c in range(N_HEADS * HEAD_DIM // LANE):
        o_ref[:, c * LANE:(c + 1) * LANE] = out_t[c * LANE:(c + 1) * LANE, :].T


def nsa_prompt(q, gate, kv, kc, vc):
    tq = NSA_Q_TILE
    qw = N_HEADS * HEAD_DIM
    ov, ex, eg = _nsa_consts()
    return pl.pallas_call(
        _nsa_prompt_body,
        grid=(kc.shape[0], SEQ // tq),
        in_specs=[pl.BlockSpec((tq, qw), lambda b, i: (i, b)),
                  pl.BlockSpec((tq, LANE), lambda b, i: (i, b)),
                  pl.BlockSpec((SEQ, 6 * KV_W), lambda b, i: (0, b)),
                  pl.BlockSpec((1, LANE, LANE), lambda b, i: (b, 0, 0)),
                  pl.BlockSpec((1, LANE, LANE), lambda b, i: (b, 0, 0)),
                  _resident(ov.shape), _resident(ex.shape), _resident(eg.shape)],
        out_specs=pl.BlockSpec((tq, qw), lambda b, i: (i, b)),
        out_shape=jax.ShapeDtypeStruct((SEQ, kc.shape[0] * qw), F32),
        scratch_shapes=[pltpu.VMEM((2, N_KV, SEQ, LANE), BF16), pltpu.VMEM((2, N_KV, SEQ, LANE), BF16),
                        pltpu.VMEM((2, N_KV, SEQ // NSA_K_TILE, LANE, NSA_K_TILE), BF16),
                        pltpu.VMEM((2, N_KV, SEQ // NSA_K_TILE, LANE, NSA_K_TILE), BF16),
                        pltpu.VMEM((HEADS_PER_KV // 2, LANE, tq), F32),
                        pltpu.VMEM((HEADS_PER_KV, tq), F32), pltpu.VMEM((HEADS_PER_KV, tq), F32),
                        pltpu.VMEM((3, qw, tq), F32)],
        compiler_params=_PARAMS2,
        name="nsa_prompt",
    )(q, gate, kv, kc, vc, ov, ex, eg)


N_PAGES = PAST_LEN // PAGE_SIZE
PAGE_CHUNKS = PAGE_SIZE // CMP_STRIDE
N_CMP_SAMPLE = (PAST_LEN + DEC_SEQ) // CMP_STRIDE - CMP_LEN // CMP_STRIDE + 1
N_SLC_SAMPLE = -(-(PAST_LEN + DEC_SEQ) // SEL_LEN)
Q_ROWS = N_KV * HEADS_PER_KV * DEC_SEQ
NEW_PAD = LANE


def _nsa_sample_consts():
    ni = np.arange(LANE)[:, None]
    sj = np.arange(LANE)[None, :]
    overlap = ((ni * CMP_STRIDE < (sj + 1) * SEL_LEN) & (ni * CMP_STRIDE + CMP_LEN > sj * SEL_LEN)
               & (ni < N_CMP_SAMPLE) & (sj < N_SLC_SAMPLE)).astype(np.float32)
    keys = np.arange(PAST_LEN)[None, :]
    expand = (keys // SEL_LEN == np.arange(LANE)[:, None]).astype(np.float32)
    r = np.arange(Q_ROWS)
    same = ((r[:, None] // (HEADS_PER_KV * DEC_SEQ) == r[None, :] // (HEADS_PER_KV * DEC_SEQ))
            & (r[:, None] % DEC_SEQ == r[None, :] % DEC_SEQ)).astype(np.float32)
    return jnp.asarray(overlap, BF16), jnp.asarray(expand, BF16), jnp.asarray(same, BF16)


def _cmp_weights(w1, w2, pe):
    half = CMP_LEN // CMP_STRIDE
    eye = jnp.eye(N_KV, dtype=F32)
    w1r = w1.reshape(half, CMP_STRIDE, HEAD_DIM, CMP_HIDDEN)
    bd = jnp.einsum('aldh,gk->algdkh', w1r, eye).reshape(half, CMP_STRIDE * KV_W, N_KV * CMP_HIDDEN)
    w1_bd = jnp.concatenate([bd[j] for j in range(half)], 1).astype(BF16)
    c1 = jnp.dot(pe.reshape(1, -1), w1, precision=lax.Precision.HIGHEST)
    w2_bd = jnp.einsum('hd,gk->ghkd', w2, eye).reshape(N_KV * CMP_HIDDEN, KV_W).astype(BF16)
    return w1_bd, jnp.tile(c1, (1, N_KV)), w2_bd


def _softmax_rows(parts, masks):
    s = [jnp.where(m, p, NEG_INF) for p, m in zip(parts, masks)]
    top = functools.reduce(jnp.maximum, [jnp.max(x, -1, keepdims=True) for x in s])
    e = [jnp.where(m, jnp.exp(x - top), 0.0) for x, m in zip(s, masks)]
    den = functools.reduce(lambda a, b: a + b, [jnp.sum(x, -1, keepdims=True) for x in e])
    return [x / jnp.maximum(den, 1e-30) for x in e]


def _nsa_sample_body(pt_ref, *refs):
    cmp_refs = refs[:N_PAGES]
    sel_refs = refs[N_PAGES:2 * N_PAGES]
    (win_ref, q_ref, new_ref, gate_ref, w1k_ref, c1k_ref, w2k_ref, w1v_ref, c1v_ref, w2v_ref,
     ov_ref, ex_ref, same_ref, o_ref, wn_ref, xs_ref) = refs[2 * N_PAGES:]
    del pt_ref
    nblk = N_PAGES * PAGE_CHUNKS
    hw = N_KV * CMP_HIDDEN

    for p, r in enumerate(cmp_refs):
        for kv in range(2):
            xs_ref[kv, p * PAGE_SIZE:(p + 1) * PAGE_SIZE, :] = r[0][kv * KV_W:(kv + 1) * KV_W, :].T

    def compress(kv, w1_ref, c1_ref, w2_ref):
        a = jnp.concatenate([xs_ref[kv, pl.ds(l, nblk, stride=CMP_STRIDE), :] for l in range(CMP_STRIDE)],
                            1).astype(BF16)
        pq = _dot(a, w1_ref[...])
        h1 = pq[:, :hw] + pltpu.roll(pq[:, hw:], nblk - 1, 0) + c1_ref[...]
        return _dot((h1 * jax.nn.sigmoid(h1)).astype(BF16), w2_ref[...]).astype(BF16)

    kc = compress(0, w1k_ref, c1k_ref, w2k_ref)
    vc = compress(1, w1v_ref, c1v_ref, w2v_ref)

    q = q_ref[0].astype(BF16)
    lane = lax.broadcasted_iota(jnp.int32, (Q_ROWS, LANE), 1)
    t_row = lax.broadcasted_iota(jnp.int32, (Q_ROWS, LANE), 0) % DEC_SEQ
    (p_cmp,) = _softmax_rows([_dot_nt(q, kc)], [lane < N_CMP_SAMPLE])
    o_cmp = _dot(p_cmp.astype(BF16), vc)

    p_hi, p_lo = _split_bf16(p_cmp)
    g_hi, g_lo = _split_bf16(_dot(same_ref[...], p_hi) + _dot(same_ref[...], p_lo))
    imp = _dot(g_hi, ov_ref[...]) + _dot(g_lo, ov_ref[...])
    q_blk = (PAST_LEN + t_row) // SEL_LEN
    forced = (lane == 0) | (lane == q_blk) | (lane == q_blk - 1)
    valid = (lane * SEL_LEN <= PAST_LEN + t_row) & (lane < N_SLC_SAMPLE)
    score = jnp.where(valid, imp + jnp.where(forced, FORCE_BONUS, 0.0), NEG_INF)
    rank = jnp.zeros((Q_ROWS, LANE), F32)
    for i in range(N_SLC_SAMPLE):
        col = score[:, i:i + 1]
        ahead = (col > score) | ((col == score) & (lane > i))
        rank = rank + jnp.where(ahead, 1.0, 0.0)
    sel = jnp.where((rank < SEL_TOPK) & (lane < N_SLC_SAMPLE), 1.0, 0.0).astype(BF16)
    chosen = _dot(sel, ex_ref[...]) > 0.5

    new = new_ref[0]
    pad_new = lambda c0: jnp.concatenate([new[:, c0:c0 + LANE], jnp.zeros((NEW_PAD - new.shape[0], LANE), F32)],
                                         0).astype(BF16)
    new_mask = (lane <= t_row) & (lane < DEC_SEQ)

    kt_sel = jnp.concatenate([r[0][0:KV_W, :] for r in sel_refs], 1).astype(BF16)
    vt_sel = jnp.concatenate([r[0][KV_W:2 * KV_W, :] for r in sel_refs], 1).astype(BF16)
    p_past, p_new = _softmax_rows([_dot(q, kt_sel), _dot_nt(q, pad_new(2 * LANE))], [chosen, new_mask])
    o_sel = _dot_nt(p_past.astype(BF16), vt_sel) + _dot(p_new.astype(BF16), pad_new(3 * LANE))

    win = win_ref[0]
    r_iota = lax.broadcasted_iota(jnp.int32, (Q_ROWS, WINDOW), 1)
    t_win = lax.broadcasted_iota(jnp.int32, (Q_ROWS, WINDOW), 0) % DEC_SEQ
    p_buf, p_new = _softmax_rows([_dot(q, win[0:KV_W, :].astype(BF16)), _dot_nt(q, pad_new(4 * LANE))],
                                 [r_iota > t_win, new_mask])
    o_win = (_dot_nt(p_buf.astype(BF16), win[KV_W:2 * KV_W, :].astype(BF16))
             + _dot(p_new.astype(BF16), pad_new(5 * LANE)))

    gate = jax.nn.sigmoid(gate_ref[0])
    o_ref[0] = gate[:, 0:1] * o_cmp + gate[:, 1:2] * o_sel + gate[:, 2:3] * o_win

    new_t = jnp.concatenate([new[:, 4 * LANE:6 * LANE], jnp.zeros((NEW_PAD - new.shape[0], 2 * KV_W), F32)], 0).T
    tail = jnp.concatenate([jnp.zeros((2 * KV_W, WINDOW - NEW_PAD), F32), new_t], 1)
    tail = pltpu.roll(tail, NEW_PAD - DEC_SEQ, 1)
    keep = lax.broadcasted_iota(jnp.int32, (2 * KV_W, WINDOW), 1) < WINDOW - DEC_SEQ
    wn_ref[0] = jnp.where(keep, pltpu.roll(win, WINDOW - DEC_SEQ, 1), tail)


def nsa_sample(layer, page_table, cache_cmp, cache_sel, win_buf, q_rows, new_kv, gate_rows, cmp_w1, cmp_w2, cmp_pos):
    n_pool = cache_cmp.shape[1]
    off = layer * n_pool
    rows_by_token = lambda a: jnp.transpose(a, (0, 1, 3, 4, 5, 2)).reshape(-1, 2 * KV_W, a.shape[2])
    cmp_view, sel_view, win_view = rows_by_token(cache_cmp), rows_by_token(cache_sel), rows_by_token(win_buf)
    wk = _cmp_weights(cmp_w1[0], cmp_w2[0], cmp_pos[0])
    wv = _cmp_weights(cmp_w1[1], cmp_w2[1], cmp_pos[1])
    consts = _nsa_sample_consts()
    page = lambda shape, p: pl.BlockSpec((1,) + shape, lambda b, pt: (off + pt[b, p], 0, 0))
    per_b = lambda shape: pl.BlockSpec((1,) + shape, lambda b, pt: (b, 0, 0))
    const = lambda a: pl.BlockSpec(a.shape, lambda b, pt: (0,) * a.ndim, pipeline_mode=pl.Buffered(1))
    weights = list(wk) + list(wv) + list(consts)
    in_specs = ([page((2 * KV_W, PAGE_SIZE), p) for p in range(N_PAGES)] * 2
                + [pl.BlockSpec((1, 2 * KV_W, WINDOW), lambda b, pt: (layer * win_buf.shape[1] + b, 0, 0)),
                   per_b((Q_ROWS, LANE)), per_b(new_kv.shape[1:]), per_b((Q_ROWS, LANE))]
                + [const(a) for a in weights])
    o, win_new = pl.pallas_call(
        _nsa_sample_body,
        grid_spec=pltpu.PrefetchScalarGridSpec(
            num_scalar_prefetch=1, grid=(q_rows.shape[0],), in_specs=in_specs,
            out_specs=[pl.BlockSpec((1, Q_ROWS, LANE), lambda b, pt: (b, 0, 0)),
                       pl.BlockSpec((1, 2 * KV_W, WINDOW), lambda b, pt: (b, 0, 0))],
            scratch_shapes=[pltpu.VMEM((2, PAST_LEN, KV_W), F32)]),
        out_shape=[jax.ShapeDtypeStruct((q_rows.shape[0], Q_ROWS, LANE), F32),
                   jax.ShapeDtypeStruct((q_rows.shape[0], 2 * KV_W, WINDOW), F32)],
        compiler_params=_PARAMS,
        name="nsa_sample",
    )(page_table, *([cmp_view] * N_PAGES), *([sel_view] * N_PAGES), win_view, q_rows, new_kv, gate_rows, *weights)
    return o, win_new.reshape(-1, 2, N_KV, HEAD_DIM, WINDOW).transpose(0, 4, 1, 2, 3)


def nsa_sample_layer(layer, q, kv, g, page_table, cache_cmp, cache_sel, win_buf, cmp_w1, cmp_w2, cmp_pos):
    nb = q.shape[0]
    eye = jnp.eye(N_KV, dtype=F32)
    by_head = lambda a, w: (a.reshape(nb, DEC_SEQ, N_KV, HEADS_PER_KV, w)
                            .transpose(0, 2, 3, 1, 4).reshape(nb, N_KV, HEADS_PER_KV * DEC_SEQ, w))
    q_s = by_head(q * ATT_SCALE, HEAD_DIM)
    q_rows = jnp.stack([q_s * eye[:, k][None, :, None, None] for k in range(N_KV)], 3).reshape(nb, Q_ROWS, LANE)
    g_s = by_head(g[..., :3 * N_HEADS], 3).reshape(nb, Q_ROWS, 3)
    gate_rows = jnp.pad(g_s, ((0, 0), (0, 0), (0, LANE - 3)))
    o, win_new = nsa_sample(layer, page_table, cache_cmp, cache_sel, win_buf, q_rows,
                            jnp.pad(kv, ((0, 0), (0, SUBLANE - DEC_SEQ), (0, 0))), gate_rows, cmp_w1, cmp_w2, cmp_pos)
    o = o.reshape(nb, N_KV, HEADS_PER_KV * DEC_SEQ, N_KV, HEAD_DIM)
    o = jnp.stack([o[:, k, :, k, :] for k in range(N_KV)], 1)
    return (o.reshape(nb, N_KV, HEADS_PER_KV, DEC_SEQ, HEAD_DIM).transpose(0, 3, 1, 2, 4)
            .reshape(nb, DEC_SEQ, N_HEADS * HEAD_DIM)), win_new


def masked_softmax(s, mask):
    p = jax.nn.softmax(jnp.where(mask, s, NEG_INF), axis=-1)
    return jnp.where(mask, p, 0.0)


def _segsum_exp(a):
    t = a.shape[-1]
    cs = jnp.cumsum(a, -1)
    tril = np.tril(np.ones((t, t), dtype=bool))
    return jnp.exp(jnp.where(tril, cs[..., :, None] - cs[..., None, :], -jnp.inf))


def _block(n, pref):
    return pref if n % pref == 0 else n


def ssd_mixer(z, xbc, dt_raw, conv_buf, h0, conv_w, conv_b, dt_bias, a_log, d_skip, norm_g):
    bt, s, _ = xbc.shape
    xpad = jnp.concatenate([conv_buf, xbc], 1)
    conv = conv_b + sum(xpad[:, k:k + s] * conv_w[k] for k in range(SSD_CONV))
    new_buf = xpad[:, s:]
    xbc_a = jax.nn.silu(conv)
    n_bc = SSD_GROUPS * SSD_STATE
    x = xbc_a[..., :SSD_INNER].reshape(bt, s, SSD_HEADS, SSD_HEAD_DIM)
    bm = xbc_a[..., SSD_INNER:SSD_INNER + n_bc]
    cm = xbc_a[..., SSD_INNER + n_bc:]
    dt = jax.nn.softplus(dt_raw + dt_bias)
    a = -jnp.exp(a_log)
    q = _block(s, SSD_CHUNK)
    nc = s // q
    r = SSD_HEADS // SSD_GROUPS
    xdt = (x * dt[..., None]).reshape(bt, nc, q, SSD_GROUPS, r, SSD_HEAD_DIM)
    bm = bm.reshape(bt, nc, q, SSD_GROUPS, SSD_STATE)
    cm = cm.reshape(bt, nc, q, SSD_GROUPS, SSD_STATE)
    a_dt = (dt * a).reshape(bt, nc, q, SSD_GROUPS, r).transpose(0, 3, 4, 1, 2)
    a_cs = jnp.cumsum(a_dt, -1)
    lmat = _segsum_exp(a_dt)
    cb = jnp.einsum('bclgn,bcsgn->bcgls', cm, bm)
    y_diag = jnp.einsum('bcgls,bgrcls,bcsgrp->bclgrp', cb, lmat, xdt)
    decay = jnp.exp(a_cs[..., -1:] - a_cs)
    states = jnp.einsum('bclgn,bgrcl,bclgrp->bcgrpn', bm, decay, xdt)
    h0g = h0.reshape(bt, 1, SSD_GROUPS, r, SSD_HEAD_DIM, SSD_STATE)
    states = jnp.concatenate([h0g, states], 1)
    chunk_decay = _segsum_exp(jnp.pad(a_cs[..., -1], [(0, 0)] * 3 + [(1, 0)]))
    states = jnp.einsum('bgrzc,bcgrpn->bzgrpn', chunk_decay, states)
    y_off = jnp.einsum('bclgn,bcgrpn,bgrcl->bclgrp', cm, states[:, :-1], jnp.exp(a_cs))
    y = (y_diag + y_off).reshape(bt, s, SSD_HEADS, SSD_HEAD_DIM) + x * d_skip[:, None]
    y = y.reshape(bt, s, SSD_INNER) * jax.nn.silu(z)
    y = _rms(y, norm_g)
    h_last = states[:, -1].reshape(bt, SSD_HEADS, SSD_HEAD_DIM, SSD_STATE)
    return y, h_last, new_buf


def compress_blocks(k, w1, w2, pe):
    bt, t = k.shape[:2]
    ratio = CMP_LEN // CMP_STRIDE
    n_chunk = t // CMP_STRIDE
    n_cmp = n_chunk - ratio + 1
    ch = k[:, :n_chunk * CMP_STRIDE].reshape(bt, n_chunk, CMP_STRIDE, N_KV, HEAD_DIM)
    blocks = jnp.concatenate([ch[:, j:j + n_cmp] for j in range(ratio)], axis=2)
    blocks = blocks + pe[:, None, :]
    flat = blocks.transpose(0, 1, 3, 2, 4).reshape(bt, n_cmp, N_KV, CMP_LEN * HEAD_DIM)
    return jax.nn.silu(flat @ w1) @ w2


def nsa_compressed(q, full_cmp, q_off, w1, w2, pe):
    s = q.shape[1]
    kc = compress_blocks(full_cmp[:, :, 0], w1[0], w2[0], pe[0])
    vc = compress_blocks(full_cmp[:, :, 1], w1[1], w2[1], pe[1])
    n_cmp = kc.shape[1]
    q_pos = q_off + np.arange(s)
    ends = np.arange(n_cmp) * CMP_STRIDE + CMP_LEN - 1
    mask = ends[None, :] <= q_pos[:, None]
    sc = jnp.einsum('bsgrd,bngd->bsgrn', q, kc).astype(jnp.float32) * ATT_SCALE
    p = masked_softmax(sc, mask[None, :, None, None, :])
    return jnp.einsum('bsgrn,bngd->bsgrd', p, vc), p


def nsa_selected(q, full_sel, p_cmp, q_off):
    bt, s = q.shape[:2]
    t = full_sel.shape[1]
    n_slc = -(-t // SEL_LEN)
    n_cmp = p_cmp.shape[-1]
    ci = np.arange(n_cmp)[:, None]
    sj = np.arange(n_slc)[None, :]
    overlap = ((ci * CMP_STRIDE < (sj + 1) * SEL_LEN)
               & (ci * CMP_STRIDE + CMP_LEN > sj * SEL_LEN)).astype(np.float32)
    imp = jnp.einsum('bsgn,nj->bsgj', p_cmp.sum(3), overlap)
    q_pos = q_off + np.arange(s)
    q_blk = q_pos // SEL_LEN
    jj = np.arange(n_slc)[None, :]
    valid = jj * SEL_LEN <= q_pos[:, None]
    forced = (jj == 0) | (jj == q_blk[:, None]) | (jj == q_blk[:, None] - 1)
    score = jnp.where(valid[None, :, None, :],
                      imp + np.where(forced, FORCE_BONUS, 0.0).astype(np.float32)[None, :, None, :],
                      NEG_INF)
    kk = min(SEL_TOPK, n_slc)
    _, idx = lax.top_k(score, kk)
    kv = jnp.pad(full_sel, ((0, 0), (0, n_slc * SEL_LEN - t), (0, 0), (0, 0), (0, 0)))
    kv = kv.reshape(bt, n_slc, SEL_LEN, 2, N_KV, HEAD_DIM).transpose(0, 4, 1, 2, 3, 5)
    kt, vt = kv[..., 0, :], kv[..., 1, :]
    qb = _block(s, SEL_QBLOCK)
    nb = s // qb
    q_blocks = q.reshape(bt, nb, qb, N_KV, HEADS_PER_KV, HEAD_DIM).swapaxes(0, 1)
    idx_blocks = idx.reshape(bt, nb, qb, N_KV, kk).swapaxes(0, 1)
    pos_blocks = jnp.asarray(q_pos.reshape(nb, qb))
    bi = jnp.arange(bt)[:, None, None, None]
    gi = jnp.arange(N_KV)[None, None, :, None]
    offs = jnp.arange(SEL_LEN)

    def one_block(args):
        qblk, iblk, pblk = args
        kg = kt[bi, gi, iblk]
        vg = vt[bi, gi, iblk]
        kpos = iblk[..., None] * SEL_LEN + offs
        mask = (kpos <= pblk[None, :, None, None, None])[:, :, :, None]
        sc = jnp.einsum('bqgrd,bqgkld->bqgrkl', qblk, kg).astype(jnp.float32) * ATT_SCALE
        shp = sc.shape
        p = masked_softmax(sc.reshape(shp[:4] + (kk * SEL_LEN,)),
                           jnp.broadcast_to(mask, shp).reshape(shp[:4] + (kk * SEL_LEN,))).reshape(shp)
        return jnp.einsum('bqgrkl,bqgkld->bqgrd', p, vg)

    o = lax.map(one_block, (q_blocks, idx_blocks, pos_blocks))
    return o.swapaxes(0, 1).reshape(bt, s, N_KV, HEADS_PER_KV, HEAD_DIM)


def nsa_window(q, full_win, n_prev):
    bt, s = q.shape[:2]
    qb = _block(s, ATT_QBLOCK)
    nb = s // qb
    band = qb + WINDOW - 1
    kv = jnp.pad(full_win, ((0, 0), (WINDOW - 1, 0), (0, 0), (0, 0), (0, 0)))
    q_blocks = q.reshape(bt, nb, qb, N_KV, HEADS_PER_KV, HEAD_DIM).swapaxes(0, 1)

    def one_block(args):
        blk, qblk = args
        start = blk * qb + n_prev
        kvb = lax.dynamic_slice_in_dim(kv, start, band, axis=1)
        k_idx = start - (WINDOW - 1) + jnp.arange(band)
        q_idx = start + jnp.arange(qb)
        rel = q_idx[:, None] - k_idx[None, :]
        mask = (k_idx[None, :] >= 0) & (rel >= 0) & (rel < WINDOW)
        sc = jnp.einsum('bqgrd,blgd->bqgrl', qblk, kvb[:, :, 0]).astype(jnp.float32) * ATT_SCALE
        p = masked_softmax(sc, mask[None, :, None, None, :])
        return jnp.einsum('bqgrl,blgd->bqgrd', p, kvb[:, :, 1])

    o = lax.map(one_block, (jnp.arange(nb), q_blocks))
    return o.swapaxes(0, 1).reshape(bt, s, N_KV, HEADS_PER_KV, HEAD_DIM)


def _kv_rows(kv, branch):
    bt, s, _ = kv.shape
    return kv[..., 2 * branch * KV_W:(2 * branch + 2) * KV_W].reshape(bt, s, 2, N_KV, HEAD_DIM)


def nsa_sample_mix(q, kv, g, past_cmp, past_sel, win_buf, cmp_w1, cmp_w2, cmp_pos):
    bt, s, _ = q.shape
    q = q.reshape(bt, s, N_KV, HEADS_PER_KV, HEAD_DIM)
    new_cmp, new_sel, new_win = (_kv_rows(kv, br) for br in range(3))
    full_cmp = jnp.concatenate([past_cmp, new_cmp], 1)
    full_sel = jnp.concatenate([past_sel, new_sel], 1)
    full_win = jnp.concatenate([win_buf, new_win], 1)
    o_cmp, p_cmp = nsa_compressed(q, full_cmp, PAST_LEN, cmp_w1, cmp_w2, cmp_pos)
    o_sel = nsa_selected(q, full_sel, p_cmp, PAST_LEN)
    o_win = nsa_window(q, full_win, win_buf.shape[1])
    gate = jax.nn.sigmoid(g[..., :3 * N_HEADS]).reshape(bt, s, N_KV, HEADS_PER_KV, 3)
    o = gate[..., 0:1] * o_cmp + gate[..., 1:2] * o_sel + gate[..., 2:3] * o_win
    keep = min(WINDOW, full_win.shape[1])
    return o.reshape(bt, s, N_HEADS * HEAD_DIM), new_cmp, new_sel, full_win[:, full_win.shape[1] - keep:]


def _prompt_to_bsw(a):
    return a.reshape(SEQ, BATCH, -1).transpose(1, 0, 2)


def _prompt_from_bsw(a):
    return a.transpose(1, 0, 2).reshape(T_PROMPT, -1)


def _sample_to_bsw(a):
    return a.reshape(SAMPLE_GROUPS, DEC_SEQ, SUBLANE, -1).transpose(0, 2, 1, 3).reshape(DEC_BATCH, DEC_SEQ, -1)


def _sample_from_bsw(a):
    return a.reshape(SAMPLE_GROUPS, SUBLANE, DEC_SEQ, -1).transpose(0, 2, 1, 3).reshape(T_SAMPLE, -1)


def _pad_cols(w, n):
    return jnp.pad(w, ((0, 0), (0, n - w.shape[1])))


def _rope_tables():
    half = ROPE_DIM // 2
    pos = jnp.concatenate([jnp.repeat(jnp.arange(SEQ), BATCH),
                           jnp.tile(jnp.repeat(PAST_LEN + jnp.arange(DEC_SEQ), SUBLANE), SAMPLE_GROUPS)])
    inv = ROPE_THETA ** (-jnp.arange(half, dtype=F32) / half)
    ang = pos.astype(F32)[:, None] * inv
    cos, sin = jnp.cos(ang), jnp.sin(ang)
    ones = jnp.ones((T_ALL, HEAD_DIM - ROPE_DIM), F32)
    zeros = jnp.zeros((T_ALL, HEAD_DIM - ROPE_DIM), F32)
    zh = jnp.zeros((T_ALL, half), F32)
    two = lambda parts: jnp.tile(jnp.concatenate(parts, -1), (1, LANE // HEAD_DIM))
    return two([cos, cos, ones]), two([zh, sin, zeros]), two([-sin, zh, zeros])


def kernel(x_prompt, x_sample, state_s5, state_ssd, state_conv, cache_cmp_kv, cache_sel_kv, state_win_kv, page_table, norm_mix_even, w_in_even, s5_a_re, s5_a_im, s5_log_dt, s5_b_re, s5_b_im, s5_c_re, s5_c_im, s5_d, s5_glu_w, s5_glu_b, ssd_conv_w, ssd_conv_b, ssd_dt_bias, ssd_a_log, ssd_d, ssd_norm, w_out_even, norm_mix_odd, w_in_odd, cmp_w1, cmp_w2, cmp_pos, w_out_odd, norm_mlp, w_up, w_down, norm_final):
    h = jnp.concatenate([_prompt_from_bsw(x_prompt), _sample_from_bsw(x_sample)], 0)
    even_widths = (S5_WIDTH, SSD_INNER, SSD_CONV_DIM, _pad_lanes(SSD_HEADS))
    odd_widths = (N_HEADS * HEAD_DIM, 6 * KV_W, _pad_lanes(3 * N_HEADS))
    rope_tabs = _rope_tables()
    rope_blocks = (tuple(range(N_HEADS * HEAD_DIM // LANE)), (0, 2, 4), ())
    outs = {k: [] for k in ("s5_p", "s5_s", "ssd_p", "ssd_s", "conv_p", "conv_s",
                            "cmp_p", "cmp_s", "sel_p", "sel_s", "win_p", "win_s")}
    for layer in range(DEPTH):
        i = layer // 2
        if layer % 2 == 0:
            w_in = _pad_cols(w_in_even[i], sum(even_widths)).astype(BF16)
            u, z, xbc, dt_raw = norm_proj(h, norm_mix_even[i], w_in, even_widths)
            tabs = s5_tables(s5_a_re[i], s5_a_im[i], s5_log_dt[i], s5_b_re[i], s5_b_im[i], s5_c_re[i], s5_c_im[i],
                             s5_d[i], s5_glu_w[i], s5_glu_b[i])
            ya_p, s5_hp = s5_scan(u[:T_PROMPT], jnp.zeros((BATCH, 2 * S5_GROUPS * S5_STATE), F32), tabs,
                                  n_seq=1, t_len=S5_PROMPT_STEPS)
            ya_s, s5_hs = s5_scan(u[T_PROMPT:], _s5_state_to_lanes(state_s5[i]), tabs,
                                  n_seq=SAMPLE_GROUPS, t_len=DEC_SEQ)
            outs["s5_p"].append(_s5_state_from_lanes(s5_hp))
            outs["s5_s"].append(_s5_state_from_lanes(s5_hs))
            ssdw = (ssd_conv_w[i], ssd_conv_b[i], ssd_dt_bias[i], ssd_a_log[i], ssd_d[i], ssd_norm[i])
            tm = lambda a: a[:T_PROMPT].reshape(SEQ, -1)
            yb_p, ssd_h, conv_new = ssd_prompt(tm(z), tm(xbc), tm(dt_raw), *ssdw)
            outs["ssd_p"].append(ssd_h.reshape(BATCH, SSD_HEADS, SSD_HEAD_DIM, SSD_STATE))
            outs["conv_p"].append(conv_new)
            yb_s, ssd_h, conv_new = ssd_mixer(
                _sample_to_bsw(z[T_PROMPT:]), _sample_to_bsw(xbc[T_PROMPT:]),
                _sample_to_bsw(dt_raw[T_PROMPT:, :SSD_HEADS]), state_conv[i], state_ssd[i], *ssdw)
            outs["ssd_s"].append(ssd_h)
            outs["conv_s"].append(conv_new)
            yb = jnp.concatenate([yb_p.reshape(T_PROMPT, -1), _sample_from_bsw(yb_s)], 0)
            h = out_proj(h, [jnp.concatenate([ya_p, ya_s], 0), yb], w_out_even[i].astype(BF16))
        else:
            w_in = _pad_cols(w_in_odd[i], sum(odd_widths)).astype(BF16)
            q, kv, g = norm_proj(h, norm_mix_odd[i], w_in, odd_widths, rope_tabs, rope_blocks)
            kv_p = _prompt_to_bsw(kv[:T_PROMPT])
            new_cmp, new_sel, new_win = (_kv_rows(kv_p, br) for br in range(3))
            kc = compress_blocks(new_cmp[:, :, 0], cmp_w1[i][0], cmp_w2[i][0], cmp_pos[i][0])
            vc = compress_blocks(new_cmp[:, :, 1], cmp_w1[i][1], cmp_w2[i][1], cmp_pos[i][1])
            pad_c = lambda a: jnp.pad(a.reshape(BATCH, N_CMP_PROMPT, KV_W), ((0, 0), (0, LANE - N_CMP_PROMPT), (0, 0)))
            y_p = nsa_prompt(q[:T_PROMPT].reshape(SEQ, -1), g[:T_PROMPT].reshape(SEQ, -1),
                             kv[:T_PROMPT].reshape(SEQ, -1), pad_c(kc), pad_c(vc)).reshape(T_PROMPT, -1)
            outs["cmp_p"].append(new_cmp)
            outs["sel_p"].append(new_sel)
            outs["win_p"].append(new_win[:, SEQ - WINDOW:])
            kv_s = _sample_to_bsw(kv[T_PROMPT:])
            y_s, win_s = nsa_sample_layer(i, _sample_to_bsw(q[T_PROMPT:]), kv_s, _sample_to_bsw(g[T_PROMPT:]),
                                          page_table, cache_cmp_kv, cache_sel_kv, state_win_kv,
                                          cmp_w1[i], cmp_w2[i], cmp_pos[i])
            outs["cmp_s"].append(_kv_rows(kv_s, 0))
            outs["sel_s"].append(_kv_rows(kv_s, 1))
            outs["win_s"].append(win_s)
            y = jnp.concatenate([y_p, _sample_from_bsw(y_s)], 0)
            h = out_proj(h, [y], w_out_odd[i].astype(BF16))
        h = mlp(h, norm_mlp[layer], w_up[layer].astype(BF16), w_down[layer].astype(BF16),
                norm_final, final=(layer == DEPTH - 1))
    y_prompt, y_sample = _prompt_to_bsw(h[:T_PROMPT]), _sample_to_bsw(h[T_PROMPT:])
    st = {k: jnp.stack(v) for k, v in outs.items()}
    return (y_prompt, y_sample, st["s5_p"], st["s5_s"], st["ssd_p"], st["ssd_s"], st["conv_p"], st["conv_s"],
            st["cmp_p"], st["cmp_s"], st["sel_p"], st["sel_s"], st["win_p"], st["win_s"])
```

```python
import functools

import jax
import jax.numpy as jnp
import numpy as np
from jax import lax
from jax.experimental import pallas as pl
from jax.experimental.pallas import tpu as pltpu

D_MODEL = 1024
BATCH = 8
SEQ = 2048
DEPTH = 4
DEC_BATCH = 128
DEC_SEQ = 4
PAST_LEN = 2048
PAGE_SIZE = 128

N_SSM = (DEPTH + 1) // 2
N_ATT = DEPTH // 2
NORM_EPS = 1e-5
D_FF = 4 * D_MODEL
NEG_INF = -1e30

S5_WIDTH = D_MODEL // 2
S5_GROUP = 16
S5_GROUPS = S5_WIDTH // S5_GROUP
S5_STATE = 64

SSD_INNER = D_MODEL
SSD_HEAD_DIM = 64
SSD_HEADS = SSD_INNER // SSD_HEAD_DIM
SSD_STATE = 128
SSD_GROUPS = 4
SSD_CONV = 4
SSD_CONV_DIM = SSD_INNER + 2 * SSD_GROUPS * SSD_STATE
SSD_CHUNK = 128
MIX_EVEN = S5_WIDTH + SSD_INNER
IN_EVEN = S5_WIDTH + SSD_INNER + SSD_CONV_DIM + SSD_HEADS

N_HEADS = 16
HEAD_DIM = D_MODEL // N_HEADS
N_KV = 2
HEADS_PER_KV = N_HEADS // N_KV
KV_W = N_KV * HEAD_DIM
ROPE_DIM = HEAD_DIM // 4
ROPE_THETA = 500000.0
ATT_SCALE = HEAD_DIM ** -0.5
CMP_LEN = 32
CMP_STRIDE = 16
CMP_HIDDEN = 2 * HEAD_DIM
SEL_LEN = 64
SEL_TOPK = 16
FORCE_BONUS = 1e4
WINDOW = 512
ATT_QBLOCK = 128
SEL_QBLOCK = 64
IN_ODD = N_HEADS * HEAD_DIM + 6 * KV_W + 3 * N_HEADS

T_PROMPT = BATCH * SEQ
T_SAMPLE = DEC_BATCH * DEC_SEQ
T_ALL = T_PROMPT + T_SAMPLE

LANE = 128
SUBLANE = 8
TOKEN_TILE = 512
FF_CHUNK = 1024
VMEM_LIMIT = 56 * 1024 * 1024

S5_SLAB_GROUPS = LANE // S5_GROUP
S5_SLABS = S5_GROUPS // S5_SLAB_GROUPS
S5_SLAB_STATE = S5_SLAB_GROUPS * S5_STATE
S5_PROMPT_STEPS = 256
SAMPLE_GROUPS = DEC_BATCH // SUBLANE

NSA_Q_TILE = 256
NSA_K_TILE = 256
N_SLC_PROMPT = SEQ // SEL_LEN
N_CMP_PROMPT = SEQ // CMP_STRIDE - CMP_LEN // CMP_STRIDE + 1

F32 = jnp.float32
BF16 = jnp.bfloat16


def _pad_lanes(n):
    return -(-n // LANE) * LANE


def _rms(x, g):
    return x * lax.rsqrt(jnp.mean(x * x, -1, keepdims=True) + NORM_EPS) * g


def _resident(shape):
    return pl.BlockSpec(shape, lambda *_: (0,) * len(shape), pipeline_mode=pl.Buffered(1))


def _rows(width):
    return pl.BlockSpec((TOKEN_TILE, width), lambda i: (i, 0))


def _dot(a, b):
    return jnp.dot(a, b, preferred_element_type=F32)


def _dot_nt(a, b):
    return lax.dot_general(a, b, (((1,), (1,)), ((), ())), preferred_element_type=F32)


def _split_bf16(x):
    hi = x.astype(BF16)
    return hi, (x - hi.astype(F32)).astype(BF16)


_PARAMS = pltpu.CompilerParams(dimension_semantics=("arbitrary",), vmem_limit_bytes=VMEM_LIMIT)
_PARAMS2 = pltpu.CompilerParams(dimension_semantics=("arbitrary", "arbitrary"), vmem_limit_bytes=VMEM_LIMIT)


def _rope_block(x, cos_f, sin_a, sin_b):
    return x * cos_f + pltpu.roll(x, ROPE_DIM // 2, 1) * sin_a + pltpu.roll(x, LANE - ROPE_DIM // 2, 1) * sin_b


def _norm_proj_body(x_ref, g_ref, w_ref, *refs, widths, rope_blocks):
    if rope_blocks is None:
        o_refs = refs
    else:
        cos_ref, sa_ref, sb_ref = refs[:3]
        o_refs = refs[3:]
    xn = _rms(x_ref[...], g_ref[...]).astype(BF16)
    off = 0
    for idx, (o_ref, wd) in enumerate(zip(o_refs, widths)):
        o_ref[...] = _dot(xn, w_ref[:, off:off + wd])
        off += wd
        if rope_blocks is not None:
            for blk in rope_blocks[idx]:
                cols = slice(blk * LANE, (blk + 1) * LANE)
                o_ref[:, cols] = _rope_block(o_ref[:, cols], cos_ref[...], sa_ref[...], sb_ref[...])


def norm_proj(x, g, w, widths, rope_tabs=None, rope_blocks=None):
    n = sum(widths)
    in_specs = [_rows(D_MODEL), _resident((1, D_MODEL)), _resident((D_MODEL, n))]
    args = [x, g.reshape(1, D_MODEL), w]
    if rope_blocks is not None:
        in_specs += [_rows(LANE)] * 3
        args += list(rope_tabs)
    return pl.pallas_call(
        functools.partial(_norm_proj_body, widths=widths, rope_blocks=rope_blocks),
        grid=(T_ALL // TOKEN_TILE,),
        in_specs=in_specs,
        out_specs=[_rows(wd) for wd in widths],
        out_shape=[jax.ShapeDtypeStruct((T_ALL, wd), F32) for wd in widths],
        compiler_params=_PARAMS,
        name="norm_proj",
    )(*args)


PROMPT_TILES = T_PROMPT // TOKEN_TILE
assert T_SAMPLE == TOKEN_TILE


def _out_proj_body(x_ref, *refs, widths):
    n = len(widths)
    yp_refs, ys_refs, w_ref, o_ref = refs[:n], refs[n:2 * n], refs[2 * n], refs[2 * n + 1]
    is_prompt = pl.program_id(0) < PROMPT_TILES
    acc = x_ref[...]
    off = 0
    for yp_ref, ys_ref, wd in zip(yp_refs, ys_refs, widths):
        y = jnp.where(is_prompt, yp_ref[...], ys_ref[...])
        acc = acc + _dot(y.astype(BF16), w_ref[off:off + wd, :])
        off += wd
    o_ref[...] = acc


def out_proj(x, parts, w):
    widths = tuple(p.shape[1] for p, _ in parts)
    prompt_rows = lambda wd: pl.BlockSpec((TOKEN_TILE, wd), lambda i: (jnp.minimum(i, PROMPT_TILES - 1), 0))
    sample_rows = lambda wd: pl.BlockSpec((TOKEN_TILE, wd), lambda i: (0, 0))
    return pl.pallas_call(
        functools.partial(_out_proj_body, widths=widths),
        grid=(T_ALL // TOKEN_TILE,),
        in_specs=([_rows(D_MODEL)] + [prompt_rows(wd) for wd in widths] + [sample_rows(wd) for wd in widths]
                  + [_resident((sum(widths), D_MODEL))]),
        out_specs=_rows(D_MODEL),
        out_shape=jax.ShapeDtypeStruct((T_ALL, D_MODEL), F32),
        compiler_params=_PARAMS,
        name="out_proj",
    )(x, *[p for p, _ in parts], *[s for _, s in parts], w)


def _mlp_body(x_ref, g_ref, wu_ref, wd_ref, gf_ref, o_ref, *, final):
    x = x_ref[...]
    xn = _rms(x, g_ref[...]).astype(BF16)
    acc = x
    for c in range(D_FF // FF_CHUNK):
        cols = slice(c * FF_CHUNK, (c + 1) * FF_CHUNK)
        h = jnp.maximum(_dot(xn, wu_ref[:, cols]), 0.0)
        acc = acc + _dot((h * h).astype(BF16), wd_ref[cols, :])
    if final:
        acc = _rms(acc, gf_ref[...])
    o_ref[...] = acc


def mlp(x, g, w_up, w_down, g_final, final):
    return pl.pallas_call(
        functools.partial(_mlp_body, final=final),
        grid=(T_ALL // TOKEN_TILE,),
        in_specs=[_rows(D_MODEL), _resident((1, D_MODEL)), _resident((D_MODEL, D_FF)),
                  _resident((D_FF, D_MODEL)), _resident((1, D_MODEL))],
        out_specs=_rows(D_MODEL),
        out_shape=jax.ShapeDtypeStruct((T_ALL, D_MODEL), F32),
        compiler_params=_PARAMS,
        name="mlp",
    )(x, g.reshape(1, D_MODEL), w_up, w_down, g_final.reshape(1, D_MODEL))


def _cmul(ar, ai, br, bi):
    return ar * br - ai * bi, ar * bi + ai * br


def s5_tables(a_re, a_im, log_dt, b_re, b_im, c_re, c_im, d_skip, glu_w, glu_b):
    dt = jnp.exp(log_dt)[:, None]
    mag = jnp.exp(a_re * dt)
    abar_re, abar_im = mag * jnp.cos(a_im * dt), mag * jnp.sin(a_im * dt)
    den = a_re * a_re + a_im * a_im
    f_re = ((abar_re - 1.0) * a_re + abar_im * a_im) / den
    f_im = (abar_im * a_re - (abar_re - 1.0) * a_im) / den
    bbar_re, bbar_im = _cmul(f_re[..., None], f_im[..., None], b_re, b_im)
    sg, ns = S5_SLAB_GROUPS, S5_SLABS
    eye = jnp.eye(sg, dtype=F32)
    a_tab = jnp.stack([abar_re.reshape(ns, S5_SLAB_STATE), abar_im.reshape(ns, S5_SLAB_STATE)], 1)

    def in_blockdiag(bb):
        return jnp.einsum('jgpc,gh->jgchp', bb.reshape(ns, sg, S5_STATE, S5_GROUP), eye).reshape(ns, LANE, S5_SLAB_STATE)

    def out_blockdiag(cc):
        return jnp.einsum('jgcp,gh->jhpgc', cc.reshape(ns, sg, S5_GROUP, S5_STATE), eye).reshape(ns, S5_SLAB_STATE, LANE)

    b_bd = jnp.concatenate([in_blockdiag(bbar_re), in_blockdiag(bbar_im)], -1).astype(BF16)
    c_bd = jnp.concatenate([out_blockdiag(c_re), out_blockdiag(-c_im)], 1).astype(BF16)
    g_bd = jnp.einsum('jgcke,gh->jgckhe', glu_w.reshape(ns, sg, S5_GROUP, 2, S5_GROUP), eye)
    g_bd = g_bd.reshape(ns, LANE, 2 * LANE).astype(BF16)
    g_b = glu_b.reshape(ns, sg, 2, S5_GROUP).transpose(0, 2, 1, 3).reshape(ns, 1, 2 * LANE)
    return a_tab, b_bd, c_bd, d_skip.reshape(ns, 1, LANE), g_bd, g_b


def _s5_body(u_ref, h0_ref, a_ref, b_ref, c_ref, d_ref, gw_ref, gb_ref, o_ref, hl_ref, st_ref, h_ref, *,
             n_seq, t_len):
    w = S5_SLAB_STATE

    @pl.when(pl.program_id(1) == 0)
    def _():
        h_ref[...] = h0_ref[...]

    ub = u_ref[...]
    st_ref[...] = _dot(ub.astype(BF16), b_ref[0])
    ar = jnp.broadcast_to(a_ref[0, 0:1, :], (SUBLANE, w))
    ai = jnp.broadcast_to(a_ref[0, 1:2, :], (SUBLANE, w))
    for s in range(n_seq):
        base = s * t_len * SUBLANE
        rows = slice(s * SUBLANE, (s + 1) * SUBLANE)

        def step(t, carry, base=base):
            hr, hi = carry
            r = pl.ds(pl.multiple_of(base + t * SUBLANE, SUBLANE), SUBLANE)
            nhr = ar * hr - ai * hi + st_ref[r, 0:w]
            nhi = ar * hi + ai * hr + st_ref[r, w:2 * w]
            st_ref[r, 0:w] = nhr
            st_ref[r, w:2 * w] = nhi
            return nhr, nhi

        hr, hi = lax.fori_loop(0, t_len, step, (h_ref[rows, 0:w], h_ref[rows, w:2 * w]), unroll=min(t_len, 8))
        h_ref[rows, 0:w] = hr
        h_ref[rows, w:2 * w] = hi
    y = _dot(st_ref[...].astype(BF16), c_ref[0]) + d_ref[0] * ub
    zg = _dot(y.astype(BF16), gw_ref[0]) + gb_ref[0]
    o_ref[...] = zg[:, :LANE] * jax.nn.sigmoid(zg[:, LANE:])
    hl_ref[...] = h_ref[...]


def s5_scan(u, h0, tabs, *, n_seq, t_len):
    blk = n_seq * t_len * SUBLANE
    n_chunks = u.shape[0] // blk
    assert n_chunks * blk == u.shape[0] and (n_seq == 1 or n_chunks == 1)
    hb = n_seq * SUBLANE
    slab = lambda shape: pl.BlockSpec((1,) + shape, lambda j, c: (j, 0, 0))
    return pl.pallas_call(
        functools.partial(_s5_body, n_seq=n_seq, t_len=t_len),
        grid=(S5_SLABS, n_chunks),
        in_specs=[pl.BlockSpec((blk, LANE), lambda j, c: (c, j)),
                  pl.BlockSpec((hb, 2 * S5_SLAB_STATE), lambda j, c: (0, j)),
                  slab((2, S5_SLAB_STATE)), slab((LANE, 2 * S5_SLAB_STATE)), slab((2 * S5_SLAB_STATE, LANE)),
                  slab((1, LANE)), slab((LANE, 2 * LANE)), slab((1, 2 * LANE))],
        out_specs=[pl.BlockSpec((blk, LANE), lambda j, c: (c, j)),
                   pl.BlockSpec((hb, 2 * S5_SLAB_STATE), lambda j, c: (0, j))],
        out_shape=[jax.ShapeDtypeStruct(u.shape, F32), jax.ShapeDtypeStruct(h0.shape, F32)],
        scratch_shapes=[pltpu.VMEM((blk, 2 * S5_SLAB_STATE), F32), pltpu.VMEM((hb, 2 * S5_SLAB_STATE), F32)],
        compiler_params=_PARAMS2,
        name="s5_scan",
    )(u, h0, *tabs)


def _s5_state_to_lanes(h):
    bt = h.shape[0]
    return h.reshape(bt, S5_SLABS, S5_SLAB_STATE, 2).transpose(0, 1, 3, 2).reshape(bt, -1)


def _s5_state_from_lanes(h):
    bt = h.shape[0]
    return h.reshape(bt, S5_SLABS, 2, S5_SLAB_STATE).transpose(0, 1, 3, 2).reshape(bt, S5_GROUPS, S5_STATE, 2)


SSD_BC = SSD_GROUPS * SSD_STATE
SSD_GROUP_W = SSD_INNER // SSD_GROUPS
CONV_PAD = SUBLANE


def _ssd_consts():
    tri = np.tril(np.ones((SSD_CHUNK, SSD_CHUNK), np.float32))
    expand = np.zeros((LANE, SSD_INNER), np.float32)
    for h in range(SSD_HEADS):
        expand[h, h * SSD_HEAD_DIM:(h + 1) * SSD_HEAD_DIM] = 1.0
    return jnp.asarray(tri, BF16), jnp.asarray(expand, BF16)


def _split3_dot(m, x):
    hi, r = _split_bf16(x)
    x2 = x - hi.astype(F32) - r.astype(F32)
    return _dot(m, hi) + _dot(m, r) + _dot(m, x2.astype(BF16))


def _expand_heads(x, eh):
    hi, lo = _split_bf16(x)
    return _dot(hi, eh) + _dot(lo, eh)


def _ssd_prompt_body(z_ref, x_ref, dt_ref, cw_ref, cb_ref, dtb_ref, alog_ref, d_ref, ng_ref, tri_ref, eh_ref,
                     y_ref, hl_ref, cl_ref, xs_ref, h_ref):
    n = SSD_CHUNK
    k0 = CONV_PAD - (SSD_CONV - 1)

    @pl.when(pl.program_id(1) == 0)
    def _():
        h_ref[...] = jnp.zeros(h_ref.shape, F32)
        xs_ref[0:CONV_PAD, :] = jnp.zeros((CONV_PAD, SSD_CONV_DIM), F32)

    xs_ref[CONV_PAD:CONV_PAD + n, :] = x_ref[...]
    conv = cb_ref[...]
    for k in range(SSD_CONV):
        conv = conv + xs_ref[k0 + k:k0 + k + n, :] * cw_ref[k:k + 1, :]
    tail = xs_ref[k0 + n:CONV_PAD + n, :]
    xs_ref[k0:CONV_PAD, :] = tail
    cl_ref[0] = tail
    xa = conv * jax.nn.sigmoid(conv)
    x = xa[:, :SSD_INNER]

    dt = jax.nn.softplus(dt_ref[...] + dtb_ref[...])
    a_dt = dt * (-jnp.exp(alog_ref[...]))
    a_cs = _split3_dot(tri_ref[...], a_dt)
    a_cs_t = a_cs.T
    dt_t = dt.T
    a_tot = a_cs[n - 1:n, :]
    eh = eh_ref[...]
    decay_in = _expand_heads(jnp.exp(a_cs), eh)
    xw = x * _expand_heads(jnp.exp(a_tot - a_cs) * dt, eh)
    lower = (lax.broadcasted_iota(jnp.int32, (n, n), 0) >= lax.broadcasted_iota(jnp.int32, (n, n), 1))
    low = lax.broadcasted_iota(jnp.int32, (n, LANE), 1) < SSD_HEAD_DIM
    heads_per_group = SSD_HEADS // SSD_GROUPS
    for g in range(SSD_GROUPS):
        b_g = xa[:, SSD_INNER + g * SSD_STATE:SSD_INNER + (g + 1) * SSD_STATE].astype(BF16)
        c_g = xa[:, SSD_INNER + SSD_BC + g * SSD_STATE:SSD_INNER + SSD_BC + (g + 1) * SSD_STATE].astype(BF16)
        gcols = slice(g * SSD_GROUP_W, (g + 1) * SSD_GROUP_W)
        h_g = h_ref[gcols, :]
        cb = _dot_nt(c_g, b_g)
        y_off = _dot_nt(c_g, h_g.astype(BF16)) * decay_in[:, gcols]
        for pr in range(heads_per_group // 2):
            cols = slice(g * SSD_GROUP_W + pr * LANE, g * SSD_GROUP_W + (pr + 1) * LANE)
            x_pair = x[:, cols]
            y_pair = y_off[:, pr * LANE:(pr + 1) * LANE] + x_pair * d_ref[:, cols]
            for side in range(2):
                h = g * heads_per_group + 2 * pr + side
                seg = jnp.where(lower, jnp.exp(a_cs[:, h:h + 1] - a_cs_t[h:h + 1, :]), 0.0)
                m = (cb * seg * dt_t[h:h + 1, :]).astype(BF16)
                x_side = jnp.where(low if side == 0 else ~low, x_pair, 0.0).astype(BF16)
                y_pair = y_pair + _dot(m, x_side)
            y_ref[:, cols] = y_pair
        upd = _dot(xw[:, gcols].T.astype(BF16), b_g)
        for hh in range(heads_per_group):
            h = g * heads_per_group + hh
            rows = slice(g * SSD_GROUP_W + hh * SSD_HEAD_DIM, g * SSD_GROUP_W + (hh + 1) * SSD_HEAD_DIM)
            scale = jnp.exp(jnp.broadcast_to(a_tot[:, h:h + 1], (SSD_HEAD_DIM, SSD_STATE)))
            h_ref[rows, :] = h_ref[rows, :] * scale + upd[hh * SSD_HEAD_DIM:(hh + 1) * SSD_HEAD_DIM, :]
    z = z_ref[...]
    y_ref[...] = _rms(y_ref[...] * (z * jax.nn.sigmoid(z)), ng_ref[...])
    hl_ref[0] = h_ref[...]


def ssd_prompt(z, xbc, dt_raw, conv_w, conv_b, dt_bias, a_log, d_skip, norm_g, nb=BATCH):
    n = SSD_CHUNK
    tri, eh = _ssd_consts()
    lanes = lambda v: jnp.pad(v, (0, LANE - v.shape[0])).reshape(1, LANE)
    chunk = lambda w: pl.BlockSpec((n, w), lambda b, c: (b * (SEQ // n) + c, 0))
    return pl.pallas_call(
        _ssd_prompt_body,
        grid=(nb, SEQ // n),
        in_specs=[chunk(SSD_INNER), chunk(SSD_CONV_DIM), chunk(LANE),
                  _resident((SSD_CONV, SSD_CONV_DIM)), _resident((1, SSD_CONV_DIM)), _resident((1, LANE)),
                  _resident((1, LANE)), _resident((1, SSD_INNER)), _resident((1, SSD_INNER)),
                  _resident(tri.shape), _resident(eh.shape)],
        out_specs=[chunk(SSD_INNER),
                   pl.BlockSpec((1, SSD_INNER, SSD_STATE), lambda b, c: (b, 0, 0)),
                   pl.BlockSpec((1, SSD_CONV - 1, SSD_CONV_DIM), lambda b, c: (b, 0, 0))],
        out_shape=[jax.ShapeDtypeStruct((nb * SEQ, SSD_INNER), F32),
                   jax.ShapeDtypeStruct((nb, SSD_INNER, SSD_STATE), F32),
                   jax.ShapeDtypeStruct((nb, SSD_CONV - 1, SSD_CONV_DIM), F32)],
        scratch_shapes=[pltpu.VMEM((CONV_PAD + n, SSD_CONV_DIM), F32), pltpu.VMEM((SSD_INNER, SSD_STATE), F32)],
        compiler_params=_PARAMS2,
        name="ssd_prompt",
    )(z, xbc, dt_raw, conv_w, conv_b.reshape(1, -1), lanes(dt_bias), lanes(a_log),
      jnp.repeat(d_skip, SSD_HEAD_DIM).reshape(1, -1), norm_g.reshape(1, -1), tri, eh)


def _nsa_consts():
    nq = N_SLC_PROMPT
    ci = np.arange(LANE)[None, :]
    sj = np.arange(nq)[:, None]
    overlap_t = ((ci * CMP_STRIDE < (sj + 1) * SEL_LEN) & (ci * CMP_STRIDE + CMP_LEN > sj * SEL_LEN)
                 & (ci < N_CMP_PROMPT)).astype(np.float32)
    keys = np.arange(SEQ)
    expand = (keys[:, None] // SEL_LEN == np.arange(nq)[None, :]).astype(np.float32)
    expand = expand.reshape(SEQ // NSA_K_TILE, NSA_K_TILE, nq)
    gate = np.zeros((3, N_HEADS * HEAD_DIM, LANE), np.float32)
    for br in range(3):
        for h in range(N_HEADS):
            gate[br, h * HEAD_DIM:(h + 1) * HEAD_DIM, h * 3 + br] = 1.0
    return jnp.asarray(overlap_t, BF16), jnp.asarray(expand, BF16), jnp.asarray(gate, BF16)


def _k_variants(k):
    low = lax.broadcasted_iota(jnp.int32, k.shape, 1) < HEAD_DIM
    k = k * ATT_SCALE
    k_sw = pltpu.roll(k, HEAD_DIM, 1)
    kl = (jnp.where(low, k, 0.0).astype(BF16), jnp.where(low, k_sw, 0.0).astype(BF16))
    kr = (jnp.where(low, 0.0, k_sw).astype(BF16), jnp.where(low, 0.0, k).astype(BF16))
    return kl, kr


def _vt_variants(v):
    vt = v.T
    zero = jnp.zeros((HEAD_DIM, v.shape[0]), F32)
    top, bot = vt[:HEAD_DIM], vt[HEAD_DIM:]
    vl = (jnp.concatenate([top, zero], 0).astype(BF16), jnp.concatenate([bot, zero], 0).astype(BF16))
    vr = (jnp.concatenate([zero, top], 0).astype(BF16), jnp.concatenate([zero, bot], 0).astype(BF16))
    return vl, vr


def _nsa_prompt_body(q_ref, g_ref, kv_ref, kc_ref, vc_ref, ov_ref, ex_ref, eg_ref, o_ref,
                     kl_ref, kr_ref, vl_ref, vr_ref, acc_ref, m_ref, l_ref, ob_ref):
    tq = NSA_Q_TILE
    tk = NSA_K_TILE
    n_pairs = HEADS_PER_KV // 2
    qi = pl.program_id(1)
    q0 = qi * tq

    @pl.when(qi == 0)
    def _():
        for br in range(2):
            c0 = (2 + 2 * br) * LANE
            kl, kr = _k_variants(kv_ref[:, c0:c0 + LANE])
            for g in range(N_KV):
                kl_ref[br, g] = kl[g]
                kr_ref[br, g] = kr[g]
            for t in range(SEQ // tk):
                vl, vr = _vt_variants(kv_ref[t * tk:(t + 1) * tk, c0 + LANE:c0 + 2 * LANE])
                for g in range(N_KV):
                    vl_ref[br, g, t] = vl[g]
                    vr_ref[br, g, t] = vr[g]

    row_low = lax.broadcasted_iota(jnp.int32, (LANE, tq), 0) < HEAD_DIM
    kpos_t = lax.broadcasted_iota(jnp.int32, (tk, tq), 0)
    qpos_t = q0 + lax.broadcasted_iota(jnp.int32, (tk, tq), 1)
    kcl, kcr = _k_variants(kc_ref[0])
    vcl, vcr = _vt_variants(vc_ref[0])
    n_iota = lax.broadcasted_iota(jnp.int32, (LANE, tq), 0)
    cmp_mask = ((n_iota * CMP_STRIDE + (CMP_LEN - 1) <= q0 + lax.broadcasted_iota(jnp.int32, (LANE, tq), 1))
                & (n_iota < N_CMP_PROMPT))

    def pair_q(g, pr):
        c0 = (g * n_pairs + pr) * LANE
        return q_ref[:, c0:c0 + LANE].astype(BF16)

    def flash(br, g, n_tiles, tile0, mask_fn, reverse):
        m_ref[...] = jnp.full(m_ref.shape, NEG_INF, F32)
        l_ref[...] = jnp.zeros(l_ref.shape, F32)
        acc_ref[...] = jnp.zeros(acc_ref.shape, F32)

        def tile(kt, carry):
            if reverse:
                kt = n_tiles - 1 - kt
            kt = tile0 + kt
            k0 = pl.multiple_of(kt * tk, tk)
            mask = mask_fn(kt, k0)
            for pr in range(n_pairs):
                qp = pair_q(g, pr)
                ps = []
                alpha = []
                for side, k_ref in enumerate((kl_ref, kr_ref)):
                    hh = 2 * pr + side
                    s = jnp.where(mask, _dot_nt(k_ref[br, g, pl.ds(k0, tk), :], qp), NEG_INF)
                    m_old = m_ref[hh:hh + 1, :]
                    m_new = jnp.maximum(m_old, jnp.max(s, 0, keepdims=True))
                    p = jnp.exp(s - m_new)
                    a = jnp.exp(m_old - m_new)
                    l_ref[hh:hh + 1, :] = a * l_ref[hh:hh + 1, :] + jnp.sum(p, 0, keepdims=True)
                    m_ref[hh:hh + 1, :] = m_new
                    ps.append(p.astype(BF16))
                    alpha.append(a)
                pv = _dot(vl_ref[br, g, kt], ps[0]) + _dot(vr_ref[br, g, kt], ps[1])
                acc_ref[pr] = acc_ref[pr] * jnp.where(row_low, alpha[0], alpha[1]) + pv
            return carry

        lax.fori_loop(0, n_tiles, tile, 0)
        for pr in range(n_pairs):
            r0 = (g * n_pairs + pr) * LANE
            l_pair = jnp.where(row_low, l_ref[2 * pr:2 * pr + 1, :], l_ref[2 * pr + 1:2 * pr + 2, :])
            ob_ref[br + 1, r0:r0 + LANE, :] = acc_ref[pr] / l_pair

    for g in range(N_KV):
        psum = jnp.zeros((LANE, tq), F32)
        for pr in range(n_pairs):
            qp = pair_q(g, pr)
            ps = []
            for kc in (kcl[g], kcr[g]):
                s = jnp.where(cmp_mask, _dot_nt(kc, qp), NEG_INF)
                p = jnp.where(cmp_mask, jnp.exp(s - jnp.max(s, 0, keepdims=True)), 0.0)
                p = p / jnp.maximum(jnp.sum(p, 0, keepdims=True), 1e-30)
                psum = psum + p
                ps.append(p.astype(BF16))
            r0 = (g * n_pairs + pr) * LANE
            ob_ref[0, r0:r0 + LANE, :] = _dot(vcl[g], ps[0]) + _dot(vcr[g], ps[1])

        p_hi, p_lo = _split_bf16(psum)
        imp = _dot(ov_ref[...], p_hi) + _dot(ov_ref[...], p_lo)
        shp = (N_SLC_PROMPT, tq)
        jj = lax.broadcasted_iota(jnp.int32, shp, 0)
        qp_t = q0 + lax.broadcasted_iota(jnp.int32, shp, 1)
        qb_t = qp_t // SEL_LEN
        forced = (jj == 0) | (jj == qb_t) | (jj == qb_t - 1)
        score = jnp.where(jj * SEL_LEN <= qp_t, imp + jnp.where(forced, FORCE_BONUS, 0.0), NEG_INF)
        rank = jnp.zeros(shp, F32)
        for i in range(N_SLC_PROMPT):
            row = score[i:i + 1, :]
            ahead = (row > score) | ((row == score) & (jj > i))
            rank = rank + jnp.where(ahead, 1.0, 0.0)
        sel_t = jnp.where(rank < SEL_TOPK, 1.0, 0.0).astype(BF16)

        def sel_mask(kt, k0, sel_t=sel_t):
            chosen = _dot(ex_ref[kt], sel_t) > 0.5
            return chosen & (k0 + kpos_t <= qpos_t)

        flash(0, g, (q0 + tq) // tk, 0, sel_mask, reverse=False)

        def win_mask(kt, k0):
            rel = qpos_t - (k0 + kpos_t)
            return (rel >= 0) & (rel < WINDOW)

        w0 = jnp.maximum(q0 - WINDOW, 0) // tk
        flash(1, g, (q0 + tq) // tk - w0, w0, win_mask, reverse=True)

    g_hi, g_lo = _split_bf16(jax.nn.sigmoid(g_ref[...]))
    out_t = jnp.zeros((N_HEADS * HEAD_DIM, tq), F32)
    for br in range(3):
        out_t = out_t + (_dot_nt(eg_ref[br], g_hi) + _dot_nt(eg_ref[br], g_lo)) * ob_ref[br]
    for c in range(N_HEADS * HEAD_DIM // LANE):
        o_ref[:, c * LANE:(c + 1) * LANE] = out_t[c * LANE:(c + 1) * LANE, :].T


def nsa_prompt(q, gate, kv, kc, vc):
    tq = NSA_Q_TILE
    qw = N_HEADS * HEAD_DIM
    tiles = SEQ // tq
    ov, ex, eg = _nsa_consts()
    return pl.pallas_call(
        _nsa_prompt_body,
        grid=(kc.shape[0], tiles),
        in_specs=[pl.BlockSpec((tq, qw), lambda b, i: (b * tiles + i, 0)),
                  pl.BlockSpec((tq, LANE), lambda b, i: (b * tiles + i, 0)),
                  pl.BlockSpec((SEQ, 6 * KV_W), lambda b, i: (b, 0)),
                  pl.BlockSpec((1, LANE, LANE), lambda b, i: (b, 0, 0)),
                  pl.BlockSpec((1, LANE, LANE), lambda b, i: (b, 0, 0)),
                  _resident(ov.shape), _resident(ex.shape), _resident(eg.shape)],
        out_specs=pl.BlockSpec((tq, qw), lambda b, i: (b * tiles + i, 0)),
        out_shape=jax.ShapeDtypeStruct((kc.shape[0] * SEQ, qw), F32),
        scratch_shapes=[pltpu.VMEM((2, N_KV, SEQ, LANE), BF16), pltpu.VMEM((2, N_KV, SEQ, LANE), BF16),
                        pltpu.VMEM((2, N_KV, SEQ // NSA_K_TILE, LANE, NSA_K_TILE), BF16),
                        pltpu.VMEM((2, N_KV, SEQ // NSA_K_TILE, LANE, NSA_K_TILE), BF16),
                        pltpu.VMEM((HEADS_PER_KV // 2, LANE, tq), F32),
                        pltpu.VMEM((HEADS_PER_KV, tq), F32), pltpu.VMEM((HEADS_PER_KV, tq), F32),
                        pltpu.VMEM((3, qw, tq), F32)],
        compiler_params=_PARAMS2,
        name="nsa_prompt",
    )(q, gate, kv, kc, vc, ov, ex, eg)


N_PAGES = PAST_LEN // PAGE_SIZE
PAGE_CHUNKS = PAGE_SIZE // CMP_STRIDE
N_CMP_SAMPLE = (PAST_LEN + DEC_SEQ) // CMP_STRIDE - CMP_LEN // CMP_STRIDE + 1
N_SLC_SAMPLE = -(-(PAST_LEN + DEC_SEQ) // SEL_LEN)
Q_ROWS = N_KV * HEADS_PER_KV * DEC_SEQ
NEW_PAD = LANE


def _nsa_sample_consts():
    ni = np.arange(LANE)[:, None]
    sj = np.arange(LANE)[None, :]
    overlap = ((ni * CMP_STRIDE < (sj + 1) * SEL_LEN) & (ni * CMP_STRIDE + CMP_LEN > sj * SEL_LEN)
               & (ni < N_CMP_SAMPLE) & (sj < N_SLC_SAMPLE)).astype(np.float32)
    keys = np.arange(PAST_LEN)[None, :]
    expand = (keys // SEL_LEN == np.arange(LANE)[:, None]).astype(np.float32)
    r = np.arange(Q_ROWS)
    same = ((r[:, None] // (HEADS_PER_KV * DEC_SEQ) == r[None, :] // (HEADS_PER_KV * DEC_SEQ))
            & (r[:, None] % DEC_SEQ == r[None, :] % DEC_SEQ)).astype(np.float32)
    return jnp.asarray(overlap, BF16), jnp.asarray(expand, BF16), jnp.asarray(same, BF16)


def _cmp_weights(w1, w2, pe):
    half = CMP_LEN // CMP_STRIDE
    eye = jnp.eye(N_KV, dtype=F32)
    w1r = w1.reshape(half, CMP_STRIDE, HEAD_DIM, CMP_HIDDEN)
    bd = jnp.einsum('aldh,gk->algdkh', w1r, eye).reshape(half, CMP_STRIDE * KV_W, N_KV * CMP_HIDDEN)
    w1_bd = jnp.concatenate([bd[j] for j in range(half)], 1).astype(BF16)
    c1 = jnp.dot(pe.reshape(1, -1), w1, precision=lax.Precision.HIGHEST)
    w2_bd = jnp.einsum('hd,gk->ghkd', w2, eye).reshape(N_KV * CMP_HIDDEN, KV_W).astype(BF16)
    return w1_bd, jnp.tile(c1, (1, N_KV)), w2_bd


def _softmax_rows(parts, masks):
    s = [jnp.where(m, p, NEG_INF) for p, m in zip(parts, masks)]
    top = functools.reduce(jnp.maximum, [jnp.max(x, -1, keepdims=True) for x in s])
    e = [jnp.where(m, jnp.exp(x - top), 0.0) for x, m in zip(s, masks)]
    den = functools.reduce(lambda a, b: a + b, [jnp.sum(x, -1, keepdims=True) for x in e])
    return [x / jnp.maximum(den, 1e-30) for x in e]


def _nsa_sample_body(pt_ref, *refs):
    cmp_refs = refs[:N_PAGES]
    sel_refs = refs[N_PAGES:2 * N_PAGES]
    (win_ref, q_ref, new_ref, gate_ref, w1k_ref, c1k_ref, w2k_ref, w1v_ref, c1v_ref, w2v_ref,
     ov_ref, ex_ref, same_ref, o_ref, wn_ref, xs_ref) = refs[2 * N_PAGES:]
    del pt_ref
    nblk = N_PAGES * PAGE_CHUNKS
    hw = N_KV * CMP_HIDDEN

    for p, r in enumerate(cmp_refs):
        for kv in range(2):
            xs_ref[kv, p * PAGE_SIZE:(p + 1) * PAGE_SIZE, :] = r[0][kv * KV_W:(kv + 1) * KV_W, :].T

    def compress(kv, w1_ref, c1_ref, w2_ref):
        a = jnp.concatenate([xs_ref[kv, pl.ds(l, nblk, stride=CMP_STRIDE), :] for l in range(CMP_STRIDE)],
                            1).astype(BF16)
        pq = _dot(a, w1_ref[...])
        h1 = pq[:, :hw] + pltpu.roll(pq[:, hw:], nblk - 1, 0) + c1_ref[...]
        return _dot((h1 * jax.nn.sigmoid(h1)).astype(BF16), w2_ref[...]).astype(BF16)

    kc = compress(0, w1k_ref, c1k_ref, w2k_ref)
    vc = compress(1, w1v_ref, c1v_ref, w2v_ref)

    q = q_ref[0].astype(BF16)
    lane = lax.broadcasted_iota(jnp.int32, (Q_ROWS, LANE), 1)
    t_row = lax.broadcasted_iota(jnp.int32, (Q_ROWS, LANE), 0) % DEC_SEQ
    (p_cmp,) = _softmax_rows([_dot_nt(q, kc)], [lane < N_CMP_SAMPLE])
    o_cmp = _dot(p_cmp.astype(BF16), vc)

    p_hi, p_lo = _split_bf16(p_cmp)
    g_hi, g_lo = _split_bf16(_dot(same_ref[...], p_hi) + _dot(same_ref[...], p_lo))
    imp = _dot(g_hi, ov_ref[...]) + _dot(g_lo, ov_ref[...])
    q_blk = (PAST_LEN + t_row) // SEL_LEN
    forced = (lane == 0) | (lane == q_blk) | (lane == q_blk - 1)
    valid = (lane * SEL_LEN <= PAST_LEN + t_row) & (lane < N_SLC_SAMPLE)
    score = jnp.where(valid, imp + jnp.where(forced, FORCE_BONUS, 0.0), NEG_INF)
    rank = jnp.zeros((Q_ROWS, LANE), F32)
    for i in range(N_SLC_SAMPLE):
        col = score[:, i:i + 1]
        ahead = (col > score) | ((col == score) & (lane > i))
        rank = rank + jnp.where(ahead, 1.0, 0.0)
    sel = jnp.where((rank < SEL_TOPK) & (lane < N_SLC_SAMPLE), 1.0, 0.0).astype(BF16)
    chosen = _dot(sel, ex_ref[...]) > 0.5

    new = new_ref[0]
    pad_new = lambda c0: jnp.concatenate([new[:, c0:c0 + LANE], jnp.zeros((NEW_PAD - new.shape[0], LANE), F32)],
                                         0).astype(BF16)
    new_mask = (lane <= t_row) & (lane < DEC_SEQ)

    kt_sel = jnp.concatenate([r[0][0:KV_W, :] for r in sel_refs], 1).astype(BF16)
    vt_sel = jnp.concatenate([r[0][KV_W:2 * KV_W, :] for r in sel_refs], 1).astype(BF16)
    p_past, p_new = _softmax_rows([_dot(q, kt_sel), _dot_nt(q, pad_new(2 * LANE))], [chosen, new_mask])
    o_sel = _dot_nt(p_past.astype(BF16), vt_sel) + _dot(p_new.astype(BF16), pad_new(3 * LANE))

    win = win_ref[0]
    r_iota = lax.broadcasted_iota(jnp.int32, (Q_ROWS, WINDOW), 1)
    t_win = lax.broadcasted_iota(jnp.int32, (Q_ROWS, WINDOW), 0) % DEC_SEQ
    p_buf, p_new = _softmax_rows([_dot(q, win[0:KV_W, :].astype(BF16)), _dot_nt(q, pad_new(4 * LANE))],
                                 [r_iota > t_win, new_mask])
    o_win = (_dot_nt(p_buf.astype(BF16), win[KV_W:2 * KV_W, :].astype(BF16))
             + _dot(p_new.astype(BF16), pad_new(5 * LANE)))

    gate = jax.nn.sigmoid(gate_ref[0])
    o_ref[0] = gate[:, 0:1] * o_cmp + gate[:, 1:2] * o_sel + gate[:, 2:3] * o_win

    new_t = jnp.concatenate([new[:, 4 * LANE:6 * LANE], jnp.zeros((NEW_PAD - new.shape[0], 2 * KV_W), F32)], 0).T
    tail = jnp.concatenate([jnp.zeros((2 * KV_W, WINDOW - NEW_PAD), F32), new_t], 1)
    tail = pltpu.roll(tail, NEW_PAD - DEC_SEQ, 1)
    keep = lax.broadcasted_iota(jnp.int32, (2 * KV_W, WINDOW), 1) < WINDOW - DEC_SEQ
    wn_ref[0] = jnp.where(keep, pltpu.roll(win, WINDOW - DEC_SEQ, 1), tail)


def nsa_sample(layer, page_table, cache_cmp, cache_sel, win_buf, q_rows, new_kv, gate_rows, cmp_w1, cmp_w2, cmp_pos):
    n_pool = cache_cmp.shape[1]
    off = layer * n_pool
    rows_by_token = lambda a: jnp.transpose(a, (0, 1, 3, 4, 5, 2)).reshape(-1, 2 * KV_W, a.shape[2])
    cmp_view, sel_view, win_view = rows_by_token(cache_cmp), rows_by_token(cache_sel), rows_by_token(win_buf)
    wk = _cmp_weights(cmp_w1[0], cmp_w2[0], cmp_pos[0])
    wv = _cmp_weights(cmp_w1[1], cmp_w2[1], cmp_pos[1])
    consts = _nsa_sample_consts()
    page = lambda shape, p: pl.BlockSpec((1,) + shape, lambda b, pt: (off + pt[b, p], 0, 0))
    per_b = lambda shape: pl.BlockSpec((1,) + shape, lambda b, pt: (b, 0, 0))
    const = lambda a: pl.BlockSpec(a.shape, lambda b, pt: (0,) * a.ndim, pipeline_mode=pl.Buffered(1))
    weights = list(wk) + list(wv) + list(consts)
    in_specs = ([page((2 * KV_W, PAGE_SIZE), p) for p in range(N_PAGES)] * 2
                + [pl.BlockSpec((1, 2 * KV_W, WINDOW), lambda b, pt: (layer * win_buf.shape[1] + b, 0, 0)),
                   per_b((Q_ROWS, LANE)), per_b(new_kv.shape[1:]), per_b((Q_ROWS, LANE))]
                + [const(a) for a in weights])
    o, win_new = pl.pallas_call(
        _nsa_sample_body,
        grid_spec=pltpu.PrefetchScalarGridSpec(
            num_scalar_prefetch=1, grid=(q_rows.shape[0],), in_specs=in_specs,
            out_specs=[pl.BlockSpec((1, Q_ROWS, LANE), lambda b, pt: (b, 0, 0)),
                       pl.BlockSpec((1, 2 * KV_W, WINDOW), lambda b, pt: (b, 0, 0))],
            scratch_shapes=[pltpu.VMEM((2, PAST_LEN, KV_W), F32)]),
        out_shape=[jax.ShapeDtypeStruct((q_rows.shape[0], Q_ROWS, LANE), F32),
                   jax.ShapeDtypeStruct((q_rows.shape[0], 2 * KV_W, WINDOW), F32)],
        compiler_params=_PARAMS,
        name="nsa_sample",
    )(page_table, *([cmp_view] * N_PAGES), *([sel_view] * N_PAGES), win_view, q_rows, new_kv, gate_rows, *weights)
    return o, win_new.reshape(-1, 2, N_KV, HEAD_DIM, WINDOW).transpose(0, 4, 1, 2, 3)


def nsa_sample_layer(layer, q, kv, g, page_table, cache_cmp, cache_sel, win_buf, cmp_w1, cmp_w2, cmp_pos):
    nb = q.shape[0]
    eye = jnp.eye(N_KV, dtype=F32)
    by_head = lambda a, w: (a.reshape(nb, DEC_SEQ, N_KV, HEADS_PER_KV, w)
                            .transpose(0, 2, 3, 1, 4).reshape(nb, N_KV, HEADS_PER_KV * DEC_SEQ, w))
    q_s = by_head(q * ATT_SCALE, HEAD_DIM)
    q_rows = jnp.stack([q_s * eye[:, k][None, :, None, None] for k in range(N_KV)], 3).reshape(nb, Q_ROWS, LANE)
    g_s = by_head(g[..., :3 * N_HEADS], 3).reshape(nb, Q_ROWS, 3)
    gate_rows = jnp.pad(g_s, ((0, 0), (0, 0), (0, LANE - 3)))
    o, win_new = nsa_sample(layer, page_table, cache_cmp, cache_sel, win_buf, q_rows,
                            jnp.pad(kv, ((0, 0), (0, SUBLANE - DEC_SEQ), (0, 0))), gate_rows, cmp_w1, cmp_w2, cmp_pos)
    o = o.reshape(nb, N_KV, HEADS_PER_KV * DEC_SEQ, N_KV, HEAD_DIM)
    o = jnp.stack([o[:, k, :, k, :] for k in range(N_KV)], 1)
    return (o.reshape(nb, N_KV, HEADS_PER_KV, DEC_SEQ, HEAD_DIM).transpose(0, 3, 1, 2, 4)
            .reshape(nb, DEC_SEQ, N_HEADS * HEAD_DIM)), win_new


def masked_softmax(s, mask):
    p = jax.nn.softmax(jnp.where(mask, s, NEG_INF), axis=-1)
    return jnp.where(mask, p, 0.0)


def _segsum_exp(a):
    t = a.shape[-1]
    cs = jnp.cumsum(a, -1)
    tril = np.tril(np.ones((t, t), dtype=bool))
    return jnp.exp(jnp.where(tril, cs[..., :, None] - cs[..., None, :], -jnp.inf))


def _block(n, pref):
    return pref if n % pref == 0 else n


def ssd_mixer(z, xbc, dt_raw, conv_buf, h0, conv_w, conv_b, dt_bias, a_log, d_skip, norm_g):
    bt, s, _ = xbc.shape
    xpad = jnp.concatenate([conv_buf, xbc], 1)
    conv = conv_b + sum(xpad[:, k:k + s] * conv_w[k] for k in range(SSD_CONV))
    new_buf = xpad[:, s:]
    xbc_a = jax.nn.silu(conv)
    n_bc = SSD_GROUPS * SSD_STATE
    x = xbc_a[..., :SSD_INNER].reshape(bt, s, SSD_HEADS, SSD_HEAD_DIM)
    bm = xbc_a[..., SSD_INNER:SSD_INNER + n_bc]
    cm = xbc_a[..., SSD_INNER + n_bc:]
    dt = jax.nn.softplus(dt_raw + dt_bias)
    a = -jnp.exp(a_log)
    q = _block(s, SSD_CHUNK)
    nc = s // q
    r = SSD_HEADS // SSD_GROUPS
    xdt = (x * dt[..., None]).reshape(bt, nc, q, SSD_GROUPS, r, SSD_HEAD_DIM)
    bm = bm.reshape(bt, nc, q, SSD_GROUPS, SSD_STATE)
    cm = cm.reshape(bt, nc, q, SSD_GROUPS, SSD_STATE)
    a_dt = (dt * a).reshape(bt, nc, q, SSD_GROUPS, r).transpose(0, 3, 4, 1, 2)
    a_cs = jnp.cumsum(a_dt, -1)
    lmat = _segsum_exp(a_dt)
    cb = jnp.einsum('bclgn,bcsgn->bcgls', cm, bm)
    y_diag = jnp.einsum('bcgls,bgrcls,bcsgrp->bclgrp', cb, lmat, xdt)
    decay = jnp.exp(a_cs[..., -1:] - a_cs)
    states = jnp.einsum('bclgn,bgrcl,bclgrp->bcgrpn', bm, decay, xdt)
    h0g = h0.reshape(bt, 1, SSD_GROUPS, r, SSD_HEAD_DIM, SSD_STATE)
    states = jnp.concatenate([h0g, states], 1)
    chunk_decay = _segsum_exp(jnp.pad(a_cs[..., -1], [(0, 0)] * 3 + [(1, 0)]))
    states = jnp.einsum('bgrzc,bcgrpn->bzgrpn', chunk_decay, states)
    y_off = jnp.einsum('bclgn,bcgrpn,bgrcl->bclgrp', cm, states[:, :-1], jnp.exp(a_cs))
    y = (y_diag + y_off).reshape(bt, s, SSD_HEADS, SSD_HEAD_DIM) + x * d_skip[:, None]
    y = y.reshape(bt, s, SSD_INNER) * jax.nn.silu(z)
    y = _rms(y, norm_g)
    h_last = states[:, -1].reshape(bt, SSD_HEADS, SSD_HEAD_DIM, SSD_STATE)
    return y, h_last, new_buf


def compress_blocks(k, w1, w2, pe):
    bt, t = k.shape[:2]
    ratio = CMP_LEN // CMP_STRIDE
    n_chunk = t // CMP_STRIDE
    n_cmp = n_chunk - ratio + 1
    ch = k[:, :n_chunk * CMP_STRIDE].reshape(bt, n_chunk, CMP_STRIDE, N_KV, HEAD_DIM)
    blocks = jnp.concatenate([ch[:, j:j + n_cmp] for j in range(ratio)], axis=2)
    blocks = blocks + pe[:, None, :]
    flat = blocks.transpose(0, 1, 3, 2, 4).reshape(bt, n_cmp, N_KV, CMP_LEN * HEAD_DIM)
    return jax.nn.silu(flat @ w1) @ w2


def nsa_compressed(q, full_cmp, q_off, w1, w2, pe):
    s = q.shape[1]
    kc = compress_blocks(full_cmp[:, :, 0], w1[0], w2[0], pe[0])
    vc = compress_blocks(full_cmp[:, :, 1], w1[1], w2[1], pe[1])
    n_cmp = kc.shape[1]
    q_pos = q_off + np.arange(s)
    ends = np.arange(n_cmp) * CMP_STRIDE + CMP_LEN - 1
    mask = ends[None, :] <= q_pos[:, None]
    sc = jnp.einsum('bsgrd,bngd->bsgrn', q, kc).astype(jnp.float32) * ATT_SCALE
    p = masked_softmax(sc, mask[None, :, None, None, :])
    return jnp.einsum('bsgrn,bngd->bsgrd', p, vc), p


def nsa_selected(q, full_sel, p_cmp, q_off):
    bt, s = q.shape[:2]
    t = full_sel.shape[1]
    n_slc = -(-t // SEL_LEN)
    n_cmp = p_cmp.shape[-1]
    ci = np.arange(n_cmp)[:, None]
    sj = np.arange(n_slc)[None, :]
    overlap = ((ci * CMP_STRIDE < (sj + 1) * SEL_LEN)
               & (ci * CMP_STRIDE + CMP_LEN > sj * SEL_LEN)).astype(np.float32)
    imp = jnp.einsum('bsgn,nj->bsgj', p_cmp.sum(3), overlap)
    q_pos = q_off + np.arange(s)
    q_blk = q_pos // SEL_LEN
    jj = np.arange(n_slc)[None, :]
    valid = jj * SEL_LEN <= q_pos[:, None]
    forced = (jj == 0) | (jj == q_blk[:, None]) | (jj == q_blk[:, None] - 1)
    score = jnp.where(valid[None, :, None, :],
                      imp + np.where(forced, FORCE_BONUS, 0.0).astype(np.float32)[None, :, None, :],
                      NEG_INF)
    kk = min(SEL_TOPK, n_slc)
    _, idx = lax.top_k(score, kk)
    kv = jnp.pad(full_sel, ((0, 0), (0, n_slc * SEL_LEN - t), (0, 0), (0, 0), (0, 0)))
    kv = kv.reshape(bt, n_slc, SEL_LEN, 2, N_KV, HEAD_DIM).transpose(0, 4, 1, 2, 3, 5)
    kt, vt = kv[..., 0, :], kv[..., 1, :]
    qb = _block(s, SEL_QBLOCK)
    nb = s // qb
    q_blocks = q.reshape(bt, nb, qb, N_KV, HEADS_PER_KV, HEAD_DIM).swapaxes(0, 1)
    idx_blocks = idx.reshape(bt, nb, qb, N_KV, kk).swapaxes(0, 1)
    pos_blocks = jnp.asarray(q_pos.reshape(nb, qb))
    bi = jnp.arange(bt)[:, None, None, None]
    gi = jnp.arange(N_KV)[None, None, :, None]
    offs = jnp.arange(SEL_LEN)

    def one_block(args):
        qblk, iblk, pblk = args
        kg = kt[bi, gi, iblk]
        vg = vt[bi, gi, iblk]
        kpos = iblk[..., None] * SEL_LEN + offs
        mask = (kpos <= pblk[None, :, None, None, None])[:, :, :, None]
        sc = jnp.einsum('bqgrd,bqgkld->bqgrkl', qblk, kg).astype(jnp.float32) * ATT_SCALE
        shp = sc.shape
        p = masked_softmax(sc.reshape(shp[:4] + (kk * SEL_LEN,)),
                           jnp.broadcast_to(mask, shp).reshape(shp[:4] + (kk * SEL_LEN,))).reshape(shp)
        return jnp.einsum('bqgrkl,bqgkld->bqgrd', p, vg)

    o = lax.map(one_block, (q_blocks, idx_blocks, pos_blocks))
    return o.swapaxes(0, 1).reshape(bt, s, N_KV, HEADS_PER_KV, HEAD_DIM)


def nsa_window(q, full_win, n_prev):
    bt, s = q.shape[:2]
    qb = _block(s, ATT_QBLOCK)
    nb = s // qb
    band = qb + WINDOW - 1
    kv = jnp.pad(full_win, ((0, 0), (WINDOW - 1, 0), (0, 0), (0, 0), (0, 0)))
    q_blocks = q.reshape(bt, nb, qb, N_KV, HEADS_PER_KV, HEAD_DIM).swapaxes(0, 1)

    def one_block(args):
        blk, qblk = args
        start = blk * qb + n_prev
        kvb = lax.dynamic_slice_in_dim(kv, start, band, axis=1)
        k_idx = start - (WINDOW - 1) + jnp.arange(band)
        q_idx = start + jnp.arange(qb)
        rel = q_idx[:, None] - k_idx[None, :]
        mask = (k_idx[None, :] >= 0) & (rel >= 0) & (rel < WINDOW)
        sc = jnp.einsum('bqgrd,blgd->bqgrl', qblk, kvb[:, :, 0]).astype(jnp.float32) * ATT_SCALE
        p = masked_softmax(sc, mask[None, :, None, None, :])
        return jnp.einsum('bqgrl,blgd->bqgrd', p, kvb[:, :, 1])

    o = lax.map(one_block, (jnp.arange(nb), q_blocks))
    return o.swapaxes(0, 1).reshape(bt, s, N_KV, HEADS_PER_KV, HEAD_DIM)


def _kv_rows(kv, branch):
    bt, s, _ = kv.shape
    return kv[..., 2 * branch * KV_W:(2 * branch + 2) * KV_W].reshape(bt, s, 2, N_KV, HEAD_DIM)


def nsa_sample_mix(q, kv, g, past_cmp, past_sel, win_buf, cmp_w1, cmp_w2, cmp_pos):
    bt, s, _ = q.shape
    q = q.reshape(bt, s, N_KV, HEADS_PER_KV, HEAD_DIM)
    new_cmp, new_sel, new_win = (_kv_rows(kv, br) for br in range(3))
    full_cmp = jnp.concatenate([past_cmp, new_cmp], 1)
    full_sel = jnp.concatenate([past_sel, new_sel], 1)
    full_win = jnp.concatenate([win_buf, new_win], 1)
    o_cmp, p_cmp = nsa_compressed(q, full_cmp, PAST_LEN, cmp_w1, cmp_w2, cmp_pos)
    o_sel = nsa_selected(q, full_sel, p_cmp, PAST_LEN)
    o_win = nsa_window(q, full_win, win_buf.shape[1])
    gate = jax.nn.sigmoid(g[..., :3 * N_HEADS]).reshape(bt, s, N_KV, HEADS_PER_KV, 3)
    o = gate[..., 0:1] * o_cmp + gate[..., 1:2] * o_sel + gate[..., 2:3] * o_win
    keep = min(WINDOW, full_win.shape[1])
    return o.reshape(bt, s, N_HEADS * HEAD_DIM), new_cmp, new_sel, full_win[:, full_win.shape[1] - keep:]


def _to_time_major(a):
    return a.reshape(BATCH, SEQ, -1).transpose(1, 0, 2).reshape(T_PROMPT, -1)


def _from_time_major(a):
    return a.reshape(SEQ, BATCH, -1).transpose(1, 0, 2).reshape(T_PROMPT, -1)


def _sample_to_bsw(a):
    return a.reshape(SAMPLE_GROUPS, DEC_SEQ, SUBLANE, -1).transpose(0, 2, 1, 3).reshape(DEC_BATCH, DEC_SEQ, -1)


def _sample_from_bsw(a):
    return a.reshape(SAMPLE_GROUPS, SUBLANE, DEC_SEQ, -1).transpose(0, 2, 1, 3).reshape(T_SAMPLE, -1)


def _pad_cols(w, n):
    return jnp.pad(w, ((0, 0), (0, n - w.shape[1])))


def _rope_tables():
    half = ROPE_DIM // 2
    pos = jnp.concatenate([jnp.tile(jnp.arange(SEQ), BATCH),
                           jnp.tile(jnp.repeat(PAST_LEN + jnp.arange(DEC_SEQ), SUBLANE), SAMPLE_GROUPS)])
    inv = ROPE_THETA ** (-jnp.arange(half, dtype=F32) / half)
    ang = pos.astype(F32)[:, None] * inv
    cos, sin = jnp.cos(ang), jnp.sin(ang)
    ones = jnp.ones((T_ALL, HEAD_DIM - ROPE_DIM), F32)
    zeros = jnp.zeros((T_ALL, HEAD_DIM - ROPE_DIM), F32)
    zh = jnp.zeros((T_ALL, half), F32)
    two = lambda parts: jnp.tile(jnp.concatenate(parts, -1), (1, LANE // HEAD_DIM))
    return two([cos, cos, ones]), two([zh, sin, zeros]), two([-sin, zh, zeros])


def kernel(x_prompt, x_sample, state_s5, state_ssd, state_conv, cache_cmp_kv, cache_sel_kv, state_win_kv, page_table, norm_mix_even, w_in_even, s5_a_re, s5_a_im, s5_log_dt, s5_b_re, s5_b_im, s5_c_re, s5_c_im, s5_d, s5_glu_w, s5_glu_b, ssd_conv_w, ssd_conv_b, ssd_dt_bias, ssd_a_log, ssd_d, ssd_norm, w_out_even, norm_mix_odd, w_in_odd, cmp_w1, cmp_w2, cmp_pos, w_out_odd, norm_mlp, w_up, w_down, norm_final):
    h = jnp.concatenate([x_prompt.reshape(T_PROMPT, D_MODEL), _sample_from_bsw(x_sample)], 0)
    even_widths = (S5_WIDTH, SSD_INNER, SSD_CONV_DIM, _pad_lanes(SSD_HEADS))
    odd_widths = (N_HEADS * HEAD_DIM, 6 * KV_W, _pad_lanes(3 * N_HEADS))
    rope_tabs = _rope_tables()
    rope_blocks = (tuple(range(N_HEADS * HEAD_DIM // LANE)), (0, 2, 4), ())
    outs = {k: [] for k in ("s5_p", "s5_s", "ssd_p", "ssd_s", "conv_p", "conv_s",
                            "cmp_p", "cmp_s", "sel_p", "sel_s", "win_p", "win_s")}
    for layer in range(DEPTH):
        i = layer // 2
        if layer % 2 == 0:
            w_in = _pad_cols(w_in_even[i], sum(even_widths)).astype(BF16)
            u, z, xbc, dt_raw = norm_proj(h, norm_mix_even[i], w_in, even_widths)
            tabs = s5_tables(s5_a_re[i], s5_a_im[i], s5_log_dt[i], s5_b_re[i], s5_b_im[i], s5_c_re[i], s5_c_im[i],
                             s5_d[i], s5_glu_w[i], s5_glu_b[i])
            ya_p, s5_hp = s5_scan(_to_time_major(u[:T_PROMPT]), jnp.zeros((BATCH, 2 * S5_GROUPS * S5_STATE), F32),
                                  tabs, n_seq=1, t_len=S5_PROMPT_STEPS)
            ya_p = _from_time_major(ya_p)
            ya_s, s5_hs = s5_scan(u[T_PROMPT:], _s5_state_to_lanes(state_s5[i]), tabs,
                                  n_seq=SAMPLE_GROUPS, t_len=DEC_SEQ)
            outs["s5_p"].append(_s5_state_from_lanes(s5_hp))
            outs["s5_s"].append(_s5_state_from_lanes(s5_hs))
            ssdw = (ssd_conv_w[i], ssd_conv_b[i], ssd_dt_bias[i], ssd_a_log[i], ssd_d[i], ssd_norm[i])
            yb_p, ssd_h, conv_new = ssd_prompt(z, xbc, dt_raw, *ssdw)
            outs["ssd_p"].append(ssd_h.reshape(BATCH, SSD_HEADS, SSD_HEAD_DIM, SSD_STATE))
            outs["conv_p"].append(conv_new)
            yb_s, ssd_h, conv_new = ssd_mixer(
                _sample_to_bsw(z[T_PROMPT:]), _sample_to_bsw(xbc[T_PROMPT:]),
                _sample_to_bsw(dt_raw[T_PROMPT:, :SSD_HEADS]), state_conv[i], state_ssd[i], *ssdw)
            outs["ssd_s"].append(ssd_h)
            outs["conv_s"].append(conv_new)
            h = out_proj(h, [(ya_p, ya_s), (yb_p, _sample_from_bsw(yb_s))], w_out_even[i].astype(BF16))
        else:
            w_in = _pad_cols(w_in_odd[i], sum(odd_widths)).astype(BF16)
            q, kv, g = norm_proj(h, norm_mix_odd[i], w_in, odd_widths, rope_tabs, rope_blocks)
            kv_p = kv[:T_PROMPT].reshape(BATCH, SEQ, -1)
            new_cmp, new_sel, new_win = (_kv_rows(kv_p, br) for br in range(3))
            kc = compress_blocks(new_cmp[:, :, 0], cmp_w1[i][0], cmp_w2[i][0], cmp_pos[i][0])
            vc = compress_blocks(new_cmp[:, :, 1], cmp_w1[i][1], cmp_w2[i][1], cmp_pos[i][1])
            pad_c = lambda a: jnp.pad(a.reshape(BATCH, N_CMP_PROMPT, KV_W), ((0, 0), (0, LANE - N_CMP_PROMPT), (0, 0)))
            y_p = nsa_prompt(q, g, kv, pad_c(kc), pad_c(vc))
            outs["cmp_p"].append(new_cmp)
            outs["sel_p"].append(new_sel)
            outs["win_p"].append(new_win[:, SEQ - WINDOW:])
            kv_s = _sample_to_bsw(kv[T_PROMPT:])
            y_s, win_s = nsa_sample_layer(i, _sample_to_bsw(q[T_PROMPT:]), kv_s, _sample_to_bsw(g[T_PROMPT:]),
                                          page_table, cache_cmp_kv, cache_sel_kv, state_win_kv,
                                          cmp_w1[i], cmp_w2[i], cmp_pos[i])
            outs["cmp_s"].append(_kv_rows(kv_s, 0))
            outs["sel_s"].append(_kv_rows(kv_s, 1))
            outs["win_s"].append(win_s)
            h = out_proj(h, [(y_p, _sample_from_bsw(y_s))], w_out_odd[i].astype(BF16))
        h = mlp(h, norm_mlp[layer], w_up[layer].astype(BF16), w_down[layer].astype(BF16),
                norm_final, final=(layer == DEPTH - 1))
    y_prompt, y_sample = h[:T_PROMPT].reshape(BATCH, SEQ, D_MODEL), _sample_to_bsw(h[T_PROMPT:])
    st = {k: jnp.stack(v) for k, v in outs.items()}
    return (y_prompt, y_sample, st["s5_p"], st["s5_s"], st["ssd_p"], st["ssd_s"], st["conv_p"], st["conv_s"],
            st["cmp_p"], st["cmp_s"], st["sel_p"], st["sel_s"], st["win_p"], st["win_s"])
```
